```python
import math
import jax, jax.numpy as jnp
from jax import lax
import numpy as np

D_MODEL = 1024
BATCH = 4
SEQ = 8192
DEPTH = 1

CHUNK = 64
D_MIX = D_MODEL
ATTN_HEADS = 8
ATTN_KV_HEADS = 2
HEAD_DIM = 64
ATTN_GROUP = ATTN_HEADS // ATTN_KV_HEADS
WINDOW = 128
WIN_CHUNKS = WINDOW // CHUNK
BAND = (WIN_CHUNKS + 1) * CHUNK
D_ATTN = ATTN_HEADS * HEAD_DIM
D_KV = ATTN_KV_HEADS * HEAD_DIM
POOL_WINDOWS = (2, 4, 8, 16)
POOL_GROUPS = len(POOL_WINDOWS)
D_POOL = D_MIX - D_ATTN
POOL_GROUP_DIM = D_POOL // POOL_GROUPS
D_IN = D_ATTN + 2 * D_KV + D_POOL
REL_BUCKETS = 32
REL_MAX_DIST = 128
PEER_KEYS = 128
PEER_EXPERTS = PEER_KEYS * PEER_KEYS
PEER_HEADS = 8
PEER_TOPK = 16
PEER_QDIM = 256
PEER_HALF = PEER_QDIM // 2
PEER_TOKEN_BLOCK = 128
EPS = 1e-6
NEG_INF = -1e30

kernel_name = "hybrid_swa_pool_peer_block"


def rms_norm(x, g):
    xf = x.astype(jnp.float32)
    y = xf * lax.rsqrt(jnp.mean(xf * xf, axis=-1, keepdims=True) + EPS)
    return (y * g.astype(jnp.float32)).astype(x.dtype)


def t5_bucket(rel):
    nb = REL_BUCKETS // 2
    max_exact = nb // 2
    base = jnp.where(rel > 0, nb, 0)
    n = jnp.abs(rel)
    nf = jnp.maximum(n, 1).astype(jnp.float32)
    large = max_exact + (jnp.log(nf / max_exact) / math.log(REL_MAX_DIST / max_exact)
                         * (nb - max_exact)).astype(jnp.int32)
    large = jnp.minimum(large, nb - 1)
    return base + jnp.where(n < max_exact, n, large)


def band_bias(rel_bias):
    i = jnp.arange(CHUNK)[:, None]
    j = jnp.arange(BAND)[None, :]
    rel = (j - WIN_CHUNKS * CHUNK) - i
    b = jnp.take(rel_bias, t5_bucket(rel), axis=0)
    b = jnp.transpose(b, (2, 0, 1)).astype(jnp.float32)
    return b.reshape(ATTN_KV_HEADS, ATTN_GROUP, CHUNK, BAND)


def to_band(t, n_chunks):
    pad = [(0, 0), (WIN_CHUNKS, 0)] + [(0, 0)] * (t.ndim - 2)
    tp = jnp.pad(t, pad)
    return jnp.concatenate([tp[:, i:i + n_chunks] for i in range(WIN_CHUNKS + 1)], axis=2)


def swa_attention(q, k, v, q_norm_g, k_norm_g, sinks, rel_bias):
    B, S, _ = q.shape
    nc = S // CHUNK
    q = rms_norm(q.reshape(B, S, ATTN_HEADS, HEAD_DIM), q_norm_g)
    k = rms_norm(k.reshape(B, S, ATTN_KV_HEADS, HEAD_DIM), k_norm_g)
    v = v.reshape(B, S, ATTN_KV_HEADS, HEAD_DIM)
    qc = q.reshape(B, nc, CHUNK, ATTN_KV_HEADS, ATTN_GROUP, HEAD_DIM)
    kb = to_band(k.reshape(B, nc, CHUNK, ATTN_KV_HEADS, HEAD_DIM), nc)
    vb = to_band(v.reshape(B, nc, CHUNK, ATTN_KV_HEADS, HEAD_DIM), nc)
    s = jnp.einsum('bcqhgd,bckhd->bchgqk', qc, kb,
                   preferred_element_type=jnp.float32) * (HEAD_DIM ** -0.5)
    s = s + band_bias(rel_bias)[None, None]
    key_chunk = jnp.arange(nc)[:, None] - WIN_CHUNKS + jnp.arange(BAND)[None, :] // CHUNK
    valid = (key_chunk >= 0)[None, :, None, None, None, :]
    s = jnp.where(valid, s, NEG_INF)
    sink = sinks.astype(jnp.float32).reshape(1, 1, ATTN_KV_HEADS, ATTN_GROUP, 1, 1)
    m = jnp.maximum(jnp.max(s, axis=-1, keepdims=True), sink)
    p = jnp.exp(s - m)
    p = p / (jnp.sum(p, axis=-1, keepdims=True) + jnp.exp(sink - m))
    o = jnp.einsum('bchgqk,bckhd->bcqhgd', p.astype(vb.dtype), vb)
    return o.reshape(B, S, D_ATTN)


def causal_mean(xg, w):
    S = xg.shape[1]
    cs = jnp.pad(jnp.cumsum(xg, axis=1), ((0, 0), (1, 0), (0, 0)))
    t = jnp.arange(S)
    lo = jnp.maximum(t + 1 - w, 0)
    total = cs[:, 1:] - jnp.take(cs, lo, axis=1)
    cnt = jnp.minimum(t + 1, w).astype(jnp.float32)
    return total / cnt[None, :, None]


def pool_mixer(p, pool_w, pool_scale):
    B, S, _ = p.shape
    pf = p.astype(jnp.float32).reshape(B, S, POOL_GROUPS, POOL_GROUP_DIM)
    pooled = jnp.stack([causal_mean(pf[:, :, g], w) for g, w in enumerate(POOL_WINDOWS)], axis=2)
    d = (pooled - pf).astype(p.dtype)
    o = jnp.einsum('bsgc,gcd->bsgd', d, pool_w)
    o = o * pool_scale.reshape(POOL_GROUPS, POOL_GROUP_DIM)
    return o.reshape(B, S, D_POOL)


def peer(h, wq, subkeys, u_tab, v_tab):
    B, S, D = h.shape
    T = B * S
    ht = h.reshape(T, D)
    q = (ht @ wq).reshape(T, PEER_HEADS, 2, PEER_HALF)
    sc = jnp.einsum('thpd,hpnd->thpn', q, subkeys, preferred_element_type=jnp.float32)
    v1, i1 = lax.top_k(sc[:, :, 0], PEER_TOPK)
    v2, i2 = lax.top_k(sc[:, :, 1], PEER_TOPK)
    cand = (v1[..., :, None] + v2[..., None, :]).reshape(T, PEER_HEADS, PEER_TOPK * PEER_TOPK)
    best, pos = lax.top_k(cand, PEER_TOPK)
    e1 = jnp.take_along_axis(i1, pos // PEER_TOPK, axis=-1)
    e2 = jnp.take_along_axis(i2, pos % PEER_TOPK, axis=-1)
    experts = (e1 * PEER_KEYS + e2).reshape(T, PEER_HEADS * PEER_TOPK)
    gates = jax.nn.softmax(best, axis=-1).reshape(T, PEER_HEADS * PEER_TOPK).astype(h.dtype)

    def block(args):
        xb, eb, gb = args
        u = jnp.take(u_tab, eb, axis=0)
        vv = jnp.take(v_tab, eb, axis=0)
        act = jax.nn.gelu(jnp.einsum('tkd,td->tk', u, xb), approximate=False)
        return jnp.einsum('tk,tkd->td', gb * act, vv)

    nb = T // PEER_TOKEN_BLOCK
    out = lax.map(block, (ht.reshape(nb, PEER_TOKEN_BLOCK, D),
                          experts.reshape(nb, PEER_TOKEN_BLOCK, -1),
                          gates.reshape(nb, PEER_TOKEN_BLOCK, -1)))
    return out.reshape(B, S, D)


def setup_inputs(seed: int = 0) -> dict:
    key = jax.random.key(seed)
    ks = jax.random.split(key, 15)
    n = jax.random.normal
    L = DEPTH
    return {
        "x": n(ks[0], (BATCH, SEQ, D_MODEL), jnp.float32),
        "norm1_g": 1.0 + 0.02 * n(ks[1], (L, D_MODEL), jnp.float32),
        "w_in": n(ks[2], (L, D_MODEL, D_IN), jnp.float32) * D_MODEL ** -0.5,
        "q_norm_g": 1.0 + 0.02 * n(ks[3], (L, HEAD_DIM), jnp.float32),
        "k_norm_g": 1.0 + 0.02 * n(ks[4], (L, HEAD_DIM), jnp.float32),
        "attn_sinks": 0.5 * n(ks[5], (L, ATTN_HEADS), jnp.float32),
        "rel_bias": 0.5 * n(ks[6], (REL_BUCKETS, ATTN_HEADS), jnp.float32),
        "pool_w": n(ks[7], (L, POOL_GROUPS, POOL_GROUP_DIM, POOL_GROUP_DIM), jnp.float32) * POOL_GROUP_DIM ** -0.5,
        "pool_scale": 1.0 + 0.02 * n(ks[8], (L, D_POOL), jnp.float32),
        "w_out": n(ks[9], (L, D_MIX, D_MODEL), jnp.float32) * D_MIX ** -0.5,
        "norm2_g": 1.0 + 0.02 * n(ks[10], (L, D_MODEL), jnp.float32),
        "peer_wq": n(ks[11], (L, D_MODEL, PEER_HEADS * PEER_QDIM), jnp.float32) * D_MODEL ** -0.5,
        "peer_subkeys": n(ks[12], (L, PEER_HEADS, 2, PEER_KEYS, PEER_HALF), jnp.float32) * PEER_HALF ** -0.5,
        "peer_u": n(ks[13], (L, PEER_EXPERTS, D_MODEL), jnp.float32) * D_MODEL ** -0.5,
        "peer_v": n(ks[14], (L, PEER_EXPERTS, D_MODEL), jnp.float32) * (PEER_HEADS * PEER_TOPK) ** -0.5,
    }


def reference(x, norm1_g, w_in, q_norm_g, k_norm_g, attn_sinks, rel_bias, pool_w, pool_scale,
              w_out, norm2_g, peer_wq, peer_subkeys, peer_u, peer_v):
    for l in range(DEPTH):
        h = rms_norm(x, norm1_g[l])
        y = h @ w_in[l]
        q = y[..., :D_ATTN]
        k = y[..., D_ATTN:D_ATTN + D_KV]
        v = y[..., D_ATTN + D_KV:D_ATTN + 2 * D_KV]
        p = y[..., D_ATTN + 2 * D_KV:]
        a = swa_attention(q, k, v, q_norm_g[l], k_norm_g[l], attn_sinks[l], rel_bias)
        b = pool_mixer(p, pool_w[l], pool_scale[l])
        x = x + jnp.concatenate([a, b], axis=-1) @ w_out[l]
        h2 = rms_norm(x, norm2_g[l])
        x = x + peer(h2, peer_wq[l], peer_subkeys[l], peer_u[l], peer_v[l])
    return x
```

```python
import functools
import math

import jax
import jax.numpy as jnp
import numpy as np
from jax import lax
from jax.experimental import pallas as pl
from jax.experimental.pallas import tpu as pltpu

F32 = jnp.float32
BF16 = jnp.bfloat16

CHUNK = 64
ATTN_HEADS = 8
ATTN_KV_HEADS = 2
HEAD_DIM = 64
ATTN_GROUP = ATTN_HEADS // ATTN_KV_HEADS
WIN_CHUNKS = 2
BAND = (WIN_CHUNKS + 1) * CHUNK
D_ATTN = ATTN_HEADS * HEAD_DIM
D_KV = ATTN_KV_HEADS * HEAD_DIM
POOL_WINDOWS = (2, 4, 8, 16)
POOL_GROUP_DIM = 128
D_POOL = len(POOL_WINDOWS) * POOL_GROUP_DIM
REL_BUCKETS = 32
REL_MAX_DIST = 128
PEER_KEYS = 128
PEER_HEADS = 8
PEER_TOPK = 16
PEER_HALF = 128
PEER_SLOTS = PEER_HEADS * PEER_TOPK
EPS = 1e-6
NEG_INF = -1e30

LANES = 128
SUBLANES = 8

IN_PROJ_ROWS = 512
MIXER_ROWS = 256
POOL_HALO = 16
KV_HALO = WIN_CHUNKS * CHUNK
ROUTE_ROWS = 256
EXPERT_BLOCK = 128
EXPERT_SUB = 16
VMEM_LIMIT = 48 * 1024 * 1024


def _rms(x, g):
    return x * lax.rsqrt(jnp.mean(x * x, axis=-1, keepdims=True) + EPS) * g


def _segment_sumsq(x, ind):
    sq = x * x
    hi = sq.astype(BF16)
    lo = (sq - hi.astype(F32)).astype(BF16)
    return (jnp.dot(hi, ind, preferred_element_type=F32)
            + jnp.dot(lo, ind, preferred_element_type=F32))


def _in_proj_kernel(x_ref, g1_ref, w_ref, gq_ref, gk_ref, indq_ref, indk_ref,
                    q_ref, kv_ref, p_ref):
    h = _rms(x_ref[...], g1_ref[...])
    y = jnp.dot(h.astype(BF16), w_ref[...], preferred_element_type=F32)
    q = y[:, :D_ATTN]
    k = y[:, D_ATTN:D_ATTN + D_KV]
    v = y[:, D_ATTN + D_KV:D_ATTN + 2 * D_KV]
    qn = q * lax.rsqrt(_segment_sumsq(q, indq_ref[...]) * (1.0 / HEAD_DIM) + EPS) * gq_ref[...]
    kn = k * lax.rsqrt(_segment_sumsq(k, indk_ref[...]) * (1.0 / HEAD_DIM) + EPS) * gk_ref[...]
    q_ref[...] = (qn * (HEAD_DIM ** -0.5)).astype(BF16)
    kv_ref[:, :D_KV] = kn.astype(BF16)
    kv_ref[:, D_KV:] = v.astype(BF16)
    p_ref[...] = y[:, D_ATTN + 2 * D_KV:]


def _in_proj(x2, g1, w_in, gq, gk):
    T, D = x2.shape
    d_in = w_in.shape[1]
    head_of = np.arange(D_ATTN) // HEAD_DIM
    indq = jnp.asarray(head_of[:, None] == head_of[None, :], BF16)
    indk = indq[:D_KV, :D_KV]
    full = lambda r, c: pl.BlockSpec((r, c), lambda i: (0, 0))
    rows = lambda c: pl.BlockSpec((IN_PROJ_ROWS, c), lambda i: (i, 0))
    return pl.pallas_call(
        _in_proj_kernel,
        grid=(T // IN_PROJ_ROWS,),
        in_specs=[rows(D), full(1, D), full(D, d_in), full(1, D_ATTN), full(1, D_KV),
                  full(D_ATTN, D_ATTN), full(D_KV, D_KV)],
        out_specs=[rows(D_ATTN), rows(2 * D_KV), rows(D_POOL)],
        out_shape=[jax.ShapeDtypeStruct((T, D_ATTN), BF16),
                   jax.ShapeDtypeStruct((T, 2 * D_KV), BF16),
                   jax.ShapeDtypeStruct((T, D_POOL), F32)],
        compiler_params=pltpu.CompilerParams(dimension_semantics=("arbitrary",),
                                             vmem_limit_bytes=VMEM_LIMIT),
        name="in_proj",
    )(x2, g1.reshape(1, D), w_in.astype(BF16),
      jnp.tile(gq, ATTN_HEADS).reshape(1, D_ATTN), jnp.tile(gk, ATTN_KV_HEADS).reshape(1, D_KV),
      indq, indk)


def _t5_bucket_table():
    i = np.arange(CHUNK)[:, None]
    j = np.arange(BAND)[None, :]
    rel = (j - WIN_CHUNKS * CHUNK) - i
    nb = REL_BUCKETS // 2
    max_exact = nb // 2
    base = np.where(rel > 0, nb, 0)
    n = np.abs(rel)
    nf = np.maximum(n, 1).astype(np.float64)
    large = max_exact + (np.log(nf / max_exact) / math.log(REL_MAX_DIST / max_exact)
                         * (nb - max_exact)).astype(np.int32)
    large = np.minimum(large, nb - 1)
    return (base + np.where(n < max_exact, n, large)).astype(np.int32)


def _mixer_kernel(x_ref, q_ref, kvc_ref, kvp_ref, pc_ref, pp_ref, bkt_ref, relb_ref, sink_ref,
                  poolw_ref, pscale_ref, wout_ref, g2_ref,
                  x1_ref, h2_ref,
                  bias_scr, kv_scr, p_scr, a_scr):
    b = pl.program_id(0)
    i = pl.program_id(1)
    rows = x_ref.shape[0]

    @pl.when((b == 0) & (i == 0))
    def _build_bias():
        bkt = bkt_ref[...]
        for h in range(ATTN_HEADS):
            acc = jnp.zeros((CHUNK, BAND), F32)
            for bucket in range(REL_BUCKETS):
                acc = jnp.where(bkt == bucket, relb_ref[bucket, h], acc)
            bias_scr[h] = acc

    kv_scr[:KV_HALO] = kvp_ref[...]
    kv_scr[KV_HALO:] = kvc_ref[...]
    p_scr[:POOL_HALO] = jnp.where(i == 0, 0.0, pp_ref[...])
    p_scr[POOL_HALO:] = pc_ref[...]

    key_slot = lax.broadcasted_iota(jnp.int32, (CHUNK, BAND), 1) // CHUNK
    for c in range(rows // CHUNK):
        first_key_chunk = i * (rows // CHUNK) + c - WIN_CHUNKS
        valid = (key_slot + first_key_chunk) >= 0
        for kvh in range(ATTN_KV_HEADS):
            kb = kv_scr[c * CHUNK:c * CHUNK + BAND, kvh * HEAD_DIM:(kvh + 1) * HEAD_DIM]
            vb = kv_scr[c * CHUNK:c * CHUNK + BAND, D_KV + kvh * HEAD_DIM:D_KV + (kvh + 1) * HEAD_DIM]
            for g in range(ATTN_GROUP):
                h = kvh * ATTN_GROUP + g
                qh = q_ref[c * CHUNK:(c + 1) * CHUNK, h * HEAD_DIM:(h + 1) * HEAD_DIM]
                s = lax.dot_general(qh, kb, (((1,), (1,)), ((), ())), preferred_element_type=F32)
                s = jnp.where(valid, s + bias_scr[h], NEG_INF)
                sink = sink_ref[h]
                m = jnp.maximum(jnp.max(s, axis=-1, keepdims=True), sink)
                e = jnp.exp(s - m)
                denom = jnp.sum(e, axis=-1, keepdims=True) + jnp.exp(sink - m)
                o = jnp.dot((e / denom).astype(BF16), vb, preferred_element_type=F32)
                a_scr[c * CHUNK:(c + 1) * CHUNK, h * HEAD_DIM:(h + 1) * HEAD_DIM] = o

    t_seq = i * rows + lax.broadcasted_iota(jnp.int32, (rows, 1), 0)
    pooled_out = []
    for g, w in enumerate(POOL_WINDOWS):
        lanes = slice(g * POOL_GROUP_DIM, (g + 1) * POOL_GROUP_DIM)
        cur = p_scr[POOL_HALO:POOL_HALO + rows, lanes]
        total = cur
        for j in range(1, w):
            total = total + p_scr[POOL_HALO - j:POOL_HALO - j + rows, lanes]
        cnt = jnp.minimum(t_seq + 1, w).astype(F32)
        d = total / cnt - cur
        og = jnp.dot(d.astype(BF16), poolw_ref[g], preferred_element_type=F32)
        pooled_out.append(og * pscale_ref[:, lanes])
    pooled = jnp.concatenate(pooled_out, axis=1)

    mixed = (jnp.dot(a_scr[...].astype(BF16), wout_ref[:D_ATTN], preferred_element_type=F32)
             + jnp.dot(pooled.astype(BF16), wout_ref[D_ATTN:], preferred_element_type=F32))
    x1 = x_ref[...] + mixed
    x1_ref[...] = x1
    h2_ref[...] = _rms(x1, g2_ref[...])


def _mixer(x2, q, kv, p, rel_bias, sinks, pool_w, pool_scale, w_out, g2, batch, seq):
    T, D = x2.shape
    nq = seq // MIXER_ROWS
    cur = lambda c: pl.BlockSpec((MIXER_ROWS, c), lambda b, i: (b * nq + i, 0))
    full2 = lambda r, c: pl.BlockSpec((r, c), lambda b, i: (0, 0))

    def prev(rows_per_block, c):
        per = MIXER_ROWS // rows_per_block
        return pl.BlockSpec((rows_per_block, c),
                            lambda b, i: (jnp.maximum((b * nq + i) * per - 1, 0), 0))

    smem = pl.BlockSpec(memory_space=pltpu.SMEM)
    return pl.pallas_call(
        _mixer_kernel,
        grid=(batch, nq),
        in_specs=[cur(D), cur(D_ATTN), cur(2 * D_KV), prev(KV_HALO, 2 * D_KV),
                  cur(D_POOL), prev(POOL_HALO, D_POOL),
                  full2(CHUNK, BAND), smem, smem,
                  pl.BlockSpec((len(POOL_WINDOWS), POOL_GROUP_DIM, POOL_GROUP_DIM),
                               lambda b, i: (0, 0, 0)),
                  full2(1, D_POOL), full2(D, D), full2(1, D)],
        out_specs=[cur(D), cur(D)],
        out_shape=[jax.ShapeDtypeStruct((T, D), F32), jax.ShapeDtypeStruct((T, D), F32)],
        scratch_shapes=[pltpu.VMEM((ATTN_HEADS, CHUNK, BAND), F32),
                        pltpu.VMEM((KV_HALO + MIXER_ROWS, 2 * D_KV), BF16),
                        pltpu.VMEM((POOL_HALO + MIXER_ROWS, D_POOL), F32),
                        pltpu.VMEM((MIXER_ROWS, D_ATTN), F32)],
        compiler_params=pltpu.CompilerParams(dimension_semantics=("arbitrary", "arbitrary"),
                                             vmem_limit_bytes=VMEM_LIMIT),
        name="mixer",
    )(x2, q, kv, kv, p, p, jnp.asarray(_t5_bucket_table()), rel_bias.astype(F32),
      sinks.astype(F32), pool_w.astype(BF16), pool_scale.reshape(1, D_POOL),
      w_out.astype(BF16), g2.reshape(1, D))


def _top_rounds(s, iota_f, store):
    big = float(s.shape[0])
    for r in range(PEER_TOPK):
        m = jnp.max(s, axis=0, keepdims=True)
        idx = jnp.min(jnp.where(s == m, iota_f, big), axis=0, keepdims=True)
        store(r, m, idx)
        s = jnp.where(iota_f == idx, -jnp.inf, s)


def _route_kernel(h2_ref, wq_ref, sk_ref, exp_ref, gate_ref, val_scr, idx_scr, best_scr, pos_scr):
    rows = h2_ref.shape[0]
    tiles = rows // LANES
    h2b = h2_ref[...].astype(BF16)
    iota_keys = lax.broadcasted_iota(jnp.int32, (PEER_KEYS, LANES), 0).astype(F32)

    def half_body(hp, carry):
        qhp = jnp.dot(h2b, wq_ref[hp], preferred_element_type=F32).astype(BF16)
        for t in range(tiles):
            sl = slice(t * LANES, (t + 1) * LANES)
            sc = lax.dot_general(sk_ref[hp], qhp[sl], (((1,), (1,)), ((), ())),
                                 preferred_element_type=F32)

            def store(r, m, idx, sl=sl):
                val_scr[hp, r:r + 1, sl] = m
                idx_scr[hp, r:r + 1, sl] = idx

            _top_rounds(sc, iota_keys, store)
        return carry

    lax.fori_loop(0, 2 * PEER_HEADS, half_body, 0)

    iota_cand = lax.broadcasted_iota(jnp.int32, (PEER_TOPK * PEER_TOPK, LANES), 0).astype(F32)

    def head_body(h, carry):
        for t in range(tiles):
            sl = slice(t * LANES, (t + 1) * LANES)
            v1 = val_scr[2 * h, :, sl]
            v2 = val_scr[2 * h + 1, :, sl]
            i1 = idx_scr[2 * h, :, sl]
            i2 = idx_scr[2 * h + 1, :, sl]
            cand = jnp.concatenate([v1[a:a + 1, :] + v2 for a in range(PEER_TOPK)], axis=0)

            def store(r, m, pos, sl=sl):
                best_scr[r:r + 1, sl] = m
                pos_scr[r:r + 1, sl] = pos

            _top_rounds(cand, iota_cand, store)
            pos = pos_scr[:, sl].astype(jnp.int32)
            pa = pos // PEER_TOPK
            pb = pos % PEER_TOPK
            e1 = jnp.zeros((PEER_TOPK, LANES), F32)
            e2 = jnp.zeros((PEER_TOPK, LANES), F32)
            for a in range(PEER_TOPK):
                e1 = e1 + jnp.where(pa == a, i1[a:a + 1, :], 0.0)
                e2 = e2 + jnp.where(pb == a, i2[a:a + 1, :], 0.0)
            best = best_scr[:, sl]
            ex = jnp.exp(best - best[0:1, :])
            slot0 = pl.multiple_of(h * PEER_TOPK, PEER_TOPK)
            exp_ref[pl.ds(slot0, PEER_TOPK), sl] = (e1 * PEER_KEYS + e2).astype(jnp.int32)
            gate_ref[pl.ds(slot0, PEER_TOPK), sl] = ex / jnp.sum(ex, axis=0, keepdims=True)
        return carry

    lax.fori_loop(0, PEER_HEADS, head_body, 0)


def _route(h2, peer_wq, peer_subkeys):
    T, D = h2.shape
    halves = 2 * PEER_HEADS
    wq = peer_wq.astype(BF16).reshape(D, halves, PEER_HALF).transpose(1, 0, 2)
    sk = peer_subkeys.astype(BF16).reshape(halves, PEER_KEYS, PEER_HALF)
    return pl.pallas_call(
        _route_kernel,
        grid=(T // ROUTE_ROWS,),
        in_specs=[pl.BlockSpec((ROUTE_ROWS, D), lambda i: (i, 0)),
                  pl.BlockSpec((halves, D, PEER_HALF), lambda i: (0, 0, 0)),
                  pl.BlockSpec((halves, PEER_KEYS, PEER_HALF), lambda i: (0, 0, 0))],
        out_specs=[pl.BlockSpec((PEER_SLOTS, ROUTE_ROWS), lambda i: (0, i)),
                   pl.BlockSpec((PEER_SLOTS, ROUTE_ROWS), lambda i: (0, i))],
        out_shape=[jax.ShapeDtypeStruct((PEER_SLOTS, T), jnp.int32),
                   jax.ShapeDtypeStruct((PEER_SLOTS, T), F32)],
        scratch_shapes=[pltpu.VMEM((halves, PEER_TOPK, ROUTE_ROWS), F32),
                        pltpu.VMEM((halves, PEER_TOPK, ROUTE_ROWS), F32),
                        pltpu.VMEM((PEER_TOPK, ROUTE_ROWS), F32),
                        pltpu.VMEM((PEER_TOPK, ROUTE_ROWS), F32)],
        compiler_params=pltpu.CompilerParams(dimension_semantics=("arbitrary",),
                                             vmem_limit_bytes=VMEM_LIMIT),
        name="route",
    )(h2, wq, sk)


def _gelu(x):
    return 0.5 * x * (1.0 + lax.erf(x * (2.0 ** -0.5)))


def _expert_kernel(exp_ref, gate_ref, h2_ref, x1_ref, u_hbm, v_hbm, out_ref,
                   ubuf, vbuf, s_scr, coef_scr, o_scr, sems):
    sub_rows = EXPERT_SUB * PEER_SLOTS

    def gather_copy(table, buf, sem, row, dst):
        return pltpu.make_async_copy(table.at[pl.ds(row, 1)], buf.at[pl.ds(dst, 1)], sem)

    def sub_block(sb, carry):
        tok0 = sb * EXPERT_SUB

        def issue_token(tt, c):
            def issue_slot(k, c2):
                row = exp_ref[k, tok0 + tt]
                dst = tt * PEER_SLOTS + k
                gather_copy(u_hbm, ubuf, sems.at[0], row, dst).start()
                gather_copy(v_hbm, vbuf, sems.at[1], row, dst).start()
                return c2
            return lax.fori_loop(0, PEER_SLOTS, issue_slot, c, unroll=8)

        lax.fori_loop(0, EXPERT_SUB, issue_token, 0)
        pltpu.make_async_copy(u_hbm.at[pl.ds(0, sub_rows)], ubuf, sems.at[0]).wait()
        pltpu.make_async_copy(v_hbm.at[pl.ds(0, sub_rows)], vbuf, sems.at[1]).wait()

        tok_rows = pl.ds(pl.multiple_of(tok0, EXPERT_SUB), EXPERT_SUB)
        hsub = h2_ref[tok_rows, :]
        for tt in range(EXPERT_SUB):
            u = ubuf[tt * PEER_SLOTS:(tt + 1) * PEER_SLOTS, :]
            s_scr[:, tt:tt + 1] = jnp.sum(u * hsub[tt:tt + 1, :], axis=1, keepdims=True)
        gates = pltpu.roll(gate_ref[...], (EXPERT_BLOCK - tok0) % EXPERT_BLOCK, 1)
        coef_scr[...] = gates * _gelu(s_scr[...])
        for tt in range(EXPERT_SUB):
            v = vbuf[tt * PEER_SLOTS:(tt + 1) * PEER_SLOTS, :]
            o_scr[tt:tt + 1, :] = jnp.sum(v * coef_scr[:, tt:tt + 1], axis=0, keepdims=True)
        out_ref[tok_rows, :] = x1_ref[tok_rows, :] + o_scr[...]
        return carry

    s_scr[...] = jnp.zeros_like(s_scr)
    lax.fori_loop(0, EXPERT_BLOCK // EXPERT_SUB, sub_block, 0)


def _experts(experts_t, gates_t, h2, x1, peer_u, peer_v):
    T, D = h2.shape
    sub_rows = EXPERT_SUB * PEER_SLOTS
    tok = lambda c: pl.BlockSpec((EXPERT_BLOCK, c), lambda i: (i, 0))
    return pl.pallas_call(
        _expert_kernel,
        grid=(T // EXPERT_BLOCK,),
        in_specs=[pl.BlockSpec((PEER_SLOTS, EXPERT_BLOCK), lambda i: (0, i),
                               memory_space=pltpu.SMEM),
                  pl.BlockSpec((PEER_SLOTS, EXPERT_BLOCK), lambda i: (0, i)),
                  tok(D), tok(D),
                  pl.BlockSpec(memory_space=pl.ANY), pl.BlockSpec(memory_space=pl.ANY)],
        out_specs=tok(D),
        out_shape=jax.ShapeDtypeStruct((T, D), F32),
        scratch_shapes=[pltpu.VMEM((sub_rows, D), F32), pltpu.VMEM((sub_rows, D), F32),
                        pltpu.VMEM((PEER_SLOTS, LANES), F32), pltpu.VMEM((PEER_SLOTS, LANES), F32),
                        pltpu.VMEM((EXPERT_SUB, D), F32),
                        pltpu.SemaphoreType.DMA((2,))],
        compiler_params=pltpu.CompilerParams(dimension_semantics=("arbitrary",),
                                             vmem_limit_bytes=VMEM_LIMIT),
        name="experts",
    )(experts_t, gates_t, h2, x1, peer_u, peer_v)


def kernel(x, norm1_g, w_in, q_norm_g, k_norm_g, attn_sinks, rel_bias, pool_w, pool_scale, w_out,
           norm2_g, peer_wq, peer_subkeys, peer_u, peer_v):
    batch, seq, d_model = x.shape
    depth = norm1_g.shape[0]
    assert seq % MIXER_ROWS == 0 and (batch * seq) % IN_PROJ_ROWS == 0
    assert (batch * seq) % ROUTE_ROWS == 0 and (batch * seq) % EXPERT_BLOCK == 0
    x2 = x.reshape(batch * seq, d_model)
    for l in range(depth):
        q, kv, p = _in_proj(x2, norm1_g[l], w_in[l], q_norm_g[l], k_norm_g[l])
        x1, h2 = _mixer(x2, q, kv, p, rel_bias, attn_sinks[l], pool_w[l], pool_scale[l],
                        w_out[l], norm2_g[l], batch, seq)
        experts_t, gates_t = _route(h2, peer_wq[l], peer_subkeys[l])
        x2 = _experts(experts_t, gates_t, h2, x1, peer_u[l], peer_v[l])
    return x2.reshape(batch, seq, d_model)
```

```python
import functools
import math

import jax
import jax.numpy as jnp
import numpy as np
from jax import lax
from jax.experimental import pallas as pl
from jax.experimental.pallas import tpu as pltpu

F32 = jnp.float32
BF16 = jnp.bfloat16

CHUNK = 64
ATTN_HEADS = 8
ATTN_KV_HEADS = 2
HEAD_DIM = 64
ATTN_GROUP = ATTN_HEADS // ATTN_KV_HEADS
WIN_CHUNKS = 2
BAND = (WIN_CHUNKS + 1) * CHUNK
D_ATTN = ATTN_HEADS * HEAD_DIM
D_KV = ATTN_KV_HEADS * HEAD_DIM
POOL_WINDOWS = (2, 4, 8, 16)
POOL_GROUP_DIM = 128
D_POOL = len(POOL_WINDOWS) * POOL_GROUP_DIM
REL_BUCKETS = 32
REL_MAX_DIST = 128
PEER_KEYS = 128
PEER_HEADS = 8
PEER_TOPK = 16
PEER_HALF = 128
PEER_SLOTS = PEER_HEADS * PEER_TOPK
EPS = 1e-6
NEG_INF = -1e30

LANES = 128
SUBLANES = 8

IN_PROJ_ROWS = 512
MIXER_ROWS = 256
POOL_HALO = 16
KV_HALO = WIN_CHUNKS * CHUNK
ROUTE_ROWS = 256
EXPERT_BLOCK = 128
EXPERT_SUB = 8
EXPERT_AHEAD = 3
EXPERT_RING = EXPERT_AHEAD + 1
VMEM_LIMIT = 48 * 1024 * 1024


def _rms(x, g):
    return x * lax.rsqrt(jnp.mean(x * x, axis=-1, keepdims=True) + EPS) * g


def _segment_sumsq(x, ind):
    sq = x * x
    hi = sq.astype(BF16)
    lo = (sq - hi.astype(F32)).astype(BF16)
    return (jnp.dot(hi, ind, preferred_element_type=F32)
            + jnp.dot(lo, ind, preferred_element_type=F32))


def _in_proj_kernel(x_ref, g1_ref, w_ref, gq_ref, gk_ref, indq_ref, indk_ref,
                    q_ref, kv_ref, p_ref):
    h = _rms(x_ref[...], g1_ref[...])
    y = jnp.dot(h.astype(BF16), w_ref[...], preferred_element_type=F32)
    q = y[:, :D_ATTN]
    k = y[:, D_ATTN:D_ATTN + D_KV]
    v = y[:, D_ATTN + D_KV:D_ATTN + 2 * D_KV]
    qn = q * lax.rsqrt(_segment_sumsq(q, indq_ref[...]) * (1.0 / HEAD_DIM) + EPS) * gq_ref[...]
    kn = k * lax.rsqrt(_segment_sumsq(k, indk_ref[...]) * (1.0 / HEAD_DIM) + EPS) * gk_ref[...]
    q_ref[...] = (qn * (HEAD_DIM ** -0.5)).astype(BF16)
    kv_ref[:, :D_KV] = kn.astype(BF16)
    kv_ref[:, D_KV:] = v.astype(BF16)
    p_ref[...] = y[:, D_ATTN + 2 * D_KV:]


def _in_proj(x2, g1, w_in, gq, gk):
    T, D = x2.shape
    d_in = w_in.shape[1]
    head_of = np.arange(D_ATTN) // HEAD_DIM
    indq = jnp.asarray(head_of[:, None] == head_of[None, :], BF16)
    indk = indq[:D_KV, :D_KV]
    full = lambda r, c: pl.BlockSpec((r, c), lambda i: (0, 0))
    rows = lambda c: pl.BlockSpec((IN_PROJ_ROWS, c), lambda i: (i, 0))
    return pl.pallas_call(
        _in_proj_kernel,
        grid=(T // IN_PROJ_ROWS,),
        in_specs=[rows(D), full(1, D), full(D, d_in), full(1, D_ATTN), full(1, D_KV),
                  full(D_ATTN, D_ATTN), full(D_KV, D_KV)],
        out_specs=[rows(D_ATTN), rows(2 * D_KV), rows(D_POOL)],
        out_shape=[jax.ShapeDtypeStruct((T, D_ATTN), BF16),
                   jax.ShapeDtypeStruct((T, 2 * D_KV), BF16),
                   jax.ShapeDtypeStruct((T, D_POOL), F32)],
        compiler_params=pltpu.CompilerParams(dimension_semantics=("arbitrary",),
                                             vmem_limit_bytes=VMEM_LIMIT),
        name="in_proj",
    )(x2, g1.reshape(1, D), w_in.astype(BF16),
      jnp.tile(gq, ATTN_HEADS).reshape(1, D_ATTN), jnp.tile(gk, ATTN_KV_HEADS).reshape(1, D_KV),
      indq, indk)


def _t5_bucket_table():
    i = np.arange(CHUNK)[:, None]
    j = np.arange(BAND)[None, :]
    rel = (j - WIN_CHUNKS * CHUNK) - i
    nb = REL_BUCKETS // 2
    max_exact = nb // 2
    base = np.where(rel > 0, nb, 0)
    n = np.abs(rel)
    nf = np.maximum(n, 1).astype(np.float64)
    large = max_exact + (np.log(nf / max_exact) / math.log(REL_MAX_DIST / max_exact)
                         * (nb - max_exact)).astype(np.int32)
    large = np.minimum(large, nb - 1)
    return (base + np.where(n < max_exact, n, large)).astype(np.int32)


def _mixer_kernel(x_ref, q_ref, kvc_ref, kvp_ref, pc_ref, pp_ref, bkt_ref, relb_ref, sink_ref,
                  poolw_ref, pscale_ref, wout_ref, g2_ref,
                  x1_ref, h2_ref,
                  bias_scr, kv_scr, p_scr, a_scr):
    b = pl.program_id(0)
    i = pl.program_id(1)
    rows = x_ref.shape[0]

    @pl.when((b == 0) & (i == 0))
    def _build_bias():
        bkt = bkt_ref[...]
        for h in range(ATTN_HEADS):
            acc = jnp.zeros((CHUNK, BAND), F32)
            for bucket in range(REL_BUCKETS):
                acc = jnp.where(bkt == bucket, relb_ref[bucket, h], acc)
            bias_scr[h] = acc

    kv_scr[:KV_HALO] = kvp_ref[...]
    kv_scr[KV_HALO:] = kvc_ref[...]
    p_scr[:POOL_HALO] = jnp.where(i == 0, 0.0, pp_ref[...])
    p_scr[POOL_HALO:] = pc_ref[...]

    key_slot = lax.broadcasted_iota(jnp.int32, (CHUNK, BAND), 1) // CHUNK
    for c in range(rows // CHUNK):
        first_key_chunk = i * (rows // CHUNK) + c - WIN_CHUNKS
        valid = (key_slot + first_key_chunk) >= 0
        for kvh in range(ATTN_KV_HEADS):
            kb = kv_scr[c * CHUNK:c * CHUNK + BAND, kvh * HEAD_DIM:(kvh + 1) * HEAD_DIM]
            vb = kv_scr[c * CHUNK:c * CHUNK + BAND, D_KV + kvh * HEAD_DIM:D_KV + (kvh + 1) * HEAD_DIM]
            for g in range(ATTN_GROUP):
                h = kvh * ATTN_GROUP + g
                qh = q_ref[c * CHUNK:(c + 1) * CHUNK, h * HEAD_DIM:(h + 1) * HEAD_DIM]
                s = lax.dot_general(qh, kb, (((1,), (1,)), ((), ())), preferred_element_type=F32)
                s = jnp.where(valid, s + bias_scr[h], NEG_INF)
                sink = sink_ref[h]
                m = jnp.maximum(jnp.max(s, axis=-1, keepdims=True), sink)
                e = jnp.exp(s - m)
                denom = jnp.sum(e, axis=-1, keepdims=True) + jnp.exp(sink - m)
                o = jnp.dot((e / denom).astype(BF16), vb, preferred_element_type=F32)
                a_scr[c * CHUNK:(c + 1) * CHUNK, h * HEAD_DIM:(h + 1) * HEAD_DIM] = o

    t_seq = i * rows + lax.broadcasted_iota(jnp.int32, (rows, 1), 0)
    pooled_out = []
    for g, w in enumerate(POOL_WINDOWS):
        lanes = slice(g * POOL_GROUP_DIM, (g + 1) * POOL_GROUP_DIM)
        cur = p_scr[POOL_HALO:POOL_HALO + rows, lanes]
        total = cur
        for j in range(1, w):
            total = total + p_scr[POOL_HALO - j:POOL_HALO - j + rows, lanes]
        cnt = jnp.minimum(t_seq + 1, w).astype(F32)
        d = total / cnt - cur
        og = jnp.dot(d.astype(BF16), poolw_ref[g], preferred_element_type=F32)
        pooled_out.append(og * pscale_ref[:, lanes])
    pooled = jnp.concatenate(pooled_out, axis=1)

    mixed = (jnp.dot(a_scr[...].astype(BF16), wout_ref[:D_ATTN], preferred_element_type=F32)
             + jnp.dot(pooled.astype(BF16), wout_ref[D_ATTN:], preferred_element_type=F32))
    x1 = x_ref[...] + mixed
    x1_ref[...] = x1
    h2_ref[...] = _rms(x1, g2_ref[...])


def _mixer(x2, q, kv, p, rel_bias, sinks, pool_w, pool_scale, w_out, g2, batch, seq):
    T, D = x2.shape
    nq = seq // MIXER_ROWS
    cur = lambda c: pl.BlockSpec((MIXER_ROWS, c), lambda b, i: (b * nq + i, 0))
    full2 = lambda r, c: pl.BlockSpec((r, c), lambda b, i: (0, 0))

    def prev(rows_per_block, c):
        per = MIXER_ROWS // rows_per_block
        return pl.BlockSpec((rows_per_block, c),
                            lambda b, i: (jnp.maximum((b * nq + i) * per - 1, 0), 0))

    smem = pl.BlockSpec(memory_space=pltpu.SMEM)
    return pl.pallas_call(
        _mixer_kernel,
        grid=(batch, nq),
        in_specs=[cur(D), cur(D_ATTN), cur(2 * D_KV), prev(KV_HALO, 2 * D_KV),
                  cur(D_POOL), prev(POOL_HALO, D_POOL),
                  full2(CHUNK, BAND), smem, smem,
                  pl.BlockSpec((len(POOL_WINDOWS), POOL_GROUP_DIM, POOL_GROUP_DIM),
                               lambda b, i: (0, 0, 0)),
                  full2(1, D_POOL), full2(D, D), full2(1, D)],
        out_specs=[cur(D), cur(D)],
        out_shape=[jax.ShapeDtypeStruct((T, D), F32), jax.ShapeDtypeStruct((T, D), F32)],
        scratch_shapes=[pltpu.VMEM((ATTN_HEADS, CHUNK, BAND), F32),
                        pltpu.VMEM((KV_HALO + MIXER_ROWS, 2 * D_KV), BF16),
                        pltpu.VMEM((POOL_HALO + MIXER_ROWS, D_POOL), F32),
                        pltpu.VMEM((MIXER_ROWS, D_ATTN), F32)],
        compiler_params=pltpu.CompilerParams(dimension_semantics=("arbitrary", "arbitrary"),
                                             vmem_limit_bytes=VMEM_LIMIT),
        name="mixer",
    )(x2, q, kv, kv, p, p, jnp.asarray(_t5_bucket_table()), rel_bias.astype(F32),
      sinks.astype(F32), pool_w.astype(BF16), pool_scale.reshape(1, D_POOL),
      w_out.astype(BF16), g2.reshape(1, D))


def _top_rounds(s, iota_f, store):
    big = float(s.shape[0])
    for r in range(PEER_TOPK):
        m = jnp.max(s, axis=0, keepdims=True)
        idx = jnp.min(jnp.where(s == m, iota_f, big), axis=0, keepdims=True)
        store(r, m, idx)
        s = jnp.where(iota_f == idx, -jnp.inf, s)


def _route_kernel(h2_ref, wq_ref, sk_ref, exp_ref, gate_ref, val_scr, idx_scr, best_scr, pos_scr):
    rows = h2_ref.shape[0]
    tiles = rows // LANES
    h2b = h2_ref[...].astype(BF16)
    iota_keys = lax.broadcasted_iota(jnp.int32, (PEER_KEYS, LANES), 0).astype(F32)

    def half_body(hp, carry):
        qhp = jnp.dot(h2b, wq_ref[hp], preferred_element_type=F32).astype(BF16)
        for t in range(tiles):
            sl = slice(t * LANES, (t + 1) * LANES)
            sc = lax.dot_general(sk_ref[hp], qhp[sl], (((1,), (1,)), ((), ())),
                                 preferred_element_type=F32)

            def store(r, m, idx, sl=sl):
                val_scr[hp, r:r + 1, sl] = m
                idx_scr[hp, r:r + 1, sl] = idx

            _top_rounds(sc, iota_keys, store)
        return carry

    lax.fori_loop(0, 2 * PEER_HEADS, half_body, 0)

    iota_cand = lax.broadcasted_iota(jnp.int32, (PEER_TOPK * PEER_TOPK, LANES), 0).astype(F32)

    def head_body(h, carry):
        for t in range(tiles):
            sl = slice(t * LANES, (t + 1) * LANES)
            v1 = val_scr[2 * h, :, sl]
            v2 = val_scr[2 * h + 1, :, sl]
            i1 = idx_scr[2 * h, :, sl]
            i2 = idx_scr[2 * h + 1, :, sl]
            cand = jnp.concatenate([v1[a:a + 1, :] + v2 for a in range(PEER_TOPK)], axis=0)

            def store(r, m, pos, sl=sl):
                best_scr[r:r + 1, sl] = m
                pos_scr[r:r + 1, sl] = pos

            _top_rounds(cand, iota_cand, store)
            pos = pos_scr[:, sl].astype(jnp.int32)
            pa = pos // PEER_TOPK
            pb = pos % PEER_TOPK
            e1 = jnp.zeros((PEER_TOPK, LANES), F32)
            e2 = jnp.zeros((PEER_TOPK, LANES), F32)
            for a in range(PEER_TOPK):
                e1 = e1 + jnp.where(pa == a, i1[a:a + 1, :], 0.0)
                e2 = e2 + jnp.where(pb == a, i2[a:a + 1, :], 0.0)
            best = best_scr[:, sl]
            ex = jnp.exp(best - best[0:1, :])
            slot0 = pl.multiple_of(h * PEER_TOPK, PEER_TOPK)
            exp_ref[pl.ds(slot0, PEER_TOPK), sl] = (e1 * PEER_KEYS + e2).astype(jnp.int32)
            gate_ref[pl.ds(slot0, PEER_TOPK), sl] = ex / jnp.sum(ex, axis=0, keepdims=True)
        return carry

    lax.fori_loop(0, PEER_HEADS, head_body, 0)


def _route(h2, peer_wq, peer_subkeys):
    T, D = h2.shape
    halves = 2 * PEER_HEADS
    wq = peer_wq.astype(BF16).reshape(D, halves, PEER_HALF).transpose(1, 0, 2)
    sk = peer_subkeys.astype(BF16).reshape(halves, PEER_KEYS, PEER_HALF)
    return pl.pallas_call(
        _route_kernel,
        grid=(T // ROUTE_ROWS,),
        in_specs=[pl.BlockSpec((ROUTE_ROWS, D), lambda i: (i, 0)),
                  pl.BlockSpec((halves, D, PEER_HALF), lambda i: (0, 0, 0)),
                  pl.BlockSpec((halves, PEER_KEYS, PEER_HALF), lambda i: (0, 0, 0))],
        out_specs=[pl.BlockSpec((PEER_SLOTS, ROUTE_ROWS), lambda i: (0, i)),
                   pl.BlockSpec((PEER_SLOTS, ROUTE_ROWS), lambda i: (0, i))],
        out_shape=[jax.ShapeDtypeStruct((PEER_SLOTS, T), jnp.int32),
                   jax.ShapeDtypeStruct((PEER_SLOTS, T), F32)],
        scratch_shapes=[pltpu.VMEM((halves, PEER_TOPK, ROUTE_ROWS), F32),
                        pltpu.VMEM((halves, PEER_TOPK, ROUTE_ROWS), F32),
                        pltpu.VMEM((PEER_TOPK, ROUTE_ROWS), F32),
                        pltpu.VMEM((PEER_TOPK, ROUTE_ROWS), F32)],
        compiler_params=pltpu.CompilerParams(dimension_semantics=("arbitrary",),
                                             vmem_limit_bytes=VMEM_LIMIT),
        name="route",
    )(h2, wq, sk)


def _gelu(x):
    return 0.5 * x * (1.0 + lax.erf(x * (2.0 ** -0.5)))


def _expert_kernel(idx_ref, gate_ref, h2_ref, x1_ref, tab_hbm, tab_tiles_hbm, out_ref, ring, sems):
    i = pl.program_id(0)
    d = h2_ref.shape[1]
    chunks = d // LANES
    groups = EXPERT_BLOCK // EXPERT_SUB
    tok_tiles = PEER_SLOTS // SUBLANES
    half = PEER_SLOTS // 2

    def issue(group, slot, tt, k0, k1):
        base = (group * EXPERT_SUB + tt) * PEER_SLOTS
        for k in range(k0, k1):
            row = idx_ref[0, 0, base + k]
            dst = ring.at[slot, tt * tok_tiles + k // SUBLANES, :, k % SUBLANES]
            pltpu.make_async_copy(tab_hbm.at[row], dst, sems.at[slot]).start(priority=k % 2)

    def wait(slot):
        pltpu.make_async_copy(tab_tiles_hbm.at[pl.ds(0, ring.shape[1])], ring.at[slot],
                              sems.at[slot]).wait()

    @pl.when(i == 0)
    def _prologue():
        for a in range(EXPERT_AHEAD):
            for tt in range(EXPERT_SUB):
                issue(a, a, tt, 0, PEER_SLOTS)

    lane = lax.broadcasted_iota(jnp.int32, (PEER_SLOTS, LANES), 1)
    sub = lax.broadcasted_iota(jnp.int32, (EXPERT_SUB, LANES), 0)

    def group_body(g, slot):
        ahead = (slot + EXPERT_AHEAD) % EXPERT_RING
        wait(slot)
        tok_rows = pl.ds(pl.multiple_of(g * EXPERT_SUB, EXPERT_SUB), EXPERT_SUB)
        hsub = h2_ref[tok_rows, :]

        def tile(tt, c):
            t0 = tt * tok_tiles
            return ring[slot, t0:t0 + tok_tiles, c].reshape(PEER_SLOTS, LANES)

        s = jnp.zeros((PEER_SLOTS, LANES), F32)
        for tt in range(EXPERT_SUB):
            issue(g + EXPERT_AHEAD, ahead, tt, 0, half)
            acc = tile(tt, 0) * hsub[tt:tt + 1, :LANES]
            for c in range(1, chunks):
                acc = acc + tile(tt, c) * hsub[tt:tt + 1, c * LANES:(c + 1) * LANES]
            s = jnp.where(lane == tt, jnp.sum(acc, axis=1, keepdims=True), s)
        gates = pltpu.roll(gate_ref[...], (EXPERT_BLOCK - g * EXPERT_SUB) % EXPERT_BLOCK, 1)
        coef = gates * _gelu(s)
        o = [jnp.zeros((EXPERT_SUB, LANES), F32) for _ in range(chunks)]
        for tt in range(EXPERT_SUB):
            issue(g + EXPERT_AHEAD, ahead, tt, half, PEER_SLOTS)
            cc = coef[:, tt:tt + 1]
            for c in range(chunks):
                oc = jnp.sum(tile(tt, chunks + c) * cc, axis=0, keepdims=True)
                o[c] = jnp.where(sub == tt, oc, o[c])
        for c in range(chunks):
            cols = slice(c * LANES, (c + 1) * LANES)
            out_ref[tok_rows, cols] = x1_ref[tok_rows, cols] + o[c]

    def ring_turn(q, carry):
        for r in range(EXPERT_RING):
            group_body(q * EXPERT_RING + r, r)
        return carry

    lax.fori_loop(0, groups // EXPERT_RING, ring_turn, 0)

    @pl.when(i == pl.num_programs(0) - 1)
    def _drain():
        for a in range(EXPERT_AHEAD):
            wait((groups + a) % EXPERT_RING)


def _experts(experts_t, gates_t, h2, x1, peer_u, peer_v):
    T, D = h2.shape
    blocks = T // EXPERT_BLOCK
    sub_rows = EXPERT_SUB * PEER_SLOTS
    ahead = EXPERT_AHEAD * sub_rows
    row_tiles = 2 * D // LANES
    n_experts = peer_u.shape[0]
    assert (EXPERT_BLOCK // EXPERT_SUB) % EXPERT_RING == 0 and n_experts % SUBLANES == 0
    ids = experts_t.T.reshape(blocks, EXPERT_BLOCK * PEER_SLOTS)
    ids = jnp.concatenate([ids, jnp.roll(ids[:, :ahead], -1, axis=0).at[-1].set(ids[-1, :ahead])],
                          axis=1).reshape(blocks, 1, EXPERT_BLOCK * PEER_SLOTS + ahead)
    table = jnp.concatenate([peer_u, peer_v], axis=1).reshape(n_experts, row_tiles, LANES)
    table_tiles = table.reshape(n_experts // SUBLANES, row_tiles, SUBLANES, LANES)
    tok = lambda c: pl.BlockSpec((EXPERT_BLOCK, c), lambda i: (i, 0))
    return pl.pallas_call(
        _expert_kernel,
        grid=(blocks,),
        in_specs=[pl.BlockSpec((1, 1, EXPERT_BLOCK * PEER_SLOTS + ahead), lambda i: (i, 0, 0),
                               memory_space=pltpu.SMEM),
                  pl.BlockSpec((PEER_SLOTS, EXPERT_BLOCK), lambda i: (0, i)),
                  tok(D), tok(D),
                  pl.BlockSpec(memory_space=pl.ANY), pl.BlockSpec(memory_space=pl.ANY)],
        out_specs=tok(D),
        out_shape=jax.ShapeDtypeStruct((T, D), F32),
        scratch_shapes=[pltpu.VMEM((EXPERT_RING, sub_rows // SUBLANES, row_tiles, SUBLANES, LANES),
                                   F32),
                        pltpu.SemaphoreType.DMA((EXPERT_RING,))],
        compiler_params=pltpu.CompilerParams(dimension_semantics=("arbitrary",),
                                             vmem_limit_bytes=VMEM_LIMIT),
        name="experts",
    )(ids, gates_t, h2, x1, table, table_tiles)


def kernel(x, norm1_g, w_in, q_norm_g, k_norm_g, attn_sinks, rel_bias, pool_w, pool_scale, w_out,
           norm2_g, peer_wq, peer_subkeys, peer_u, peer_v):
    batch, seq, d_model = x.shape
    depth = norm1_g.shape[0]
    assert seq % MIXER_ROWS == 0 and (batch * seq) % IN_PROJ_ROWS == 0
    assert (batch * seq) % ROUTE_ROWS == 0 and (batch * seq) % EXPERT_BLOCK == 0
    x2 = x.reshape(batch * seq, d_model)
    for l in range(depth):
        q, kv, p = _in_proj(x2, norm1_g[l], w_in[l], q_norm_g[l], k_norm_g[l])
        x1, h2 = _mixer(x2, q, kv, p, rel_bias, attn_sinks[l], pool_w[l], pool_scale[l],
                        w_out[l], norm2_g[l], batch, seq)
        experts_t, gates_t = _route(h2, peer_wq[l], peer_subkeys[l])
        x2 = _experts(experts_t, gates_t, h2, x1, peer_u[l], peer_v[l])
    return x2.reshape(batch, seq, d_model)
```

```python
import functools
import math

import jax
import jax.numpy as jnp
import numpy as np
from jax import lax
from jax.experimental import pallas as pl
from jax.experimental.pallas import tpu as pltpu

F32 = jnp.float32
BF16 = jnp.bfloat16

CHUNK = 64
ATTN_HEADS = 8
ATTN_KV_HEADS = 2
HEAD_DIM = 64
ATTN_GROUP = ATTN_HEADS // ATTN_KV_HEADS
WIN_CHUNKS = 2
BAND = (WIN_CHUNKS + 1) * CHUNK
D_ATTN = ATTN_HEADS * HEAD_DIM
D_KV = ATTN_KV_HEADS * HEAD_DIM
POOL_WINDOWS = (2, 4, 8, 16)
POOL_GROUP_DIM = 128
D_POOL = len(POOL_WINDOWS) * POOL_GROUP_DIM
REL_BUCKETS = 32
REL_MAX_DIST = 128
PEER_KEYS = 128
PEER_HEADS = 8
PEER_TOPK = 16
PEER_HALF = 128
PEER_SLOTS = PEER_HEADS * PEER_TOPK
EPS = 1e-6
NEG_INF = -1e30

LANES = 128
SUBLANES = 8

IN_PROJ_ROWS = 512
MIXER_ROWS = 256
POOL_HALO = 16
KV_HALO = WIN_CHUNKS * CHUNK
ROUTE_ROWS = 256
EXPERT_BLOCK = 128
EXPERT_SUB = 8
EXPERT_AHEAD = 3
EXPERT_RING = EXPERT_AHEAD + 1
VMEM_LIMIT = 48 * 1024 * 1024


def _rms(x, g):
    return x * lax.rsqrt(jnp.mean(x * x, axis=-1, keepdims=True) + EPS) * g


def _segment_sumsq(x, ind):
    sq = x * x
    hi = sq.astype(BF16)
    lo = (sq - hi.astype(F32)).astype(BF16)
    return (jnp.dot(hi, ind, preferred_element_type=F32)
            + jnp.dot(lo, ind, preferred_element_type=F32))


def _in_proj_kernel(x_ref, g1_ref, w_ref, gq_ref, gk_ref, indq_ref, indk_ref,
                    q_ref, kv_ref, p_ref):
    h = _rms(x_ref[...], g1_ref[...])
    y = jnp.dot(h.astype(BF16), w_ref[...], preferred_element_type=F32)
    q = y[:, :D_ATTN]
    k = y[:, D_ATTN:D_ATTN + D_KV]
    v = y[:, D_ATTN + D_KV:D_ATTN + 2 * D_KV]
    qn = q * lax.rsqrt(_segment_sumsq(q, indq_ref[...]) * (1.0 / HEAD_DIM) + EPS) * gq_ref[...]
    kn = k * lax.rsqrt(_segment_sumsq(k, indk_ref[...]) * (1.0 / HEAD_DIM) + EPS) * gk_ref[...]
    q_ref[...] = (qn * (HEAD_DIM ** -0.5)).astype(BF16)
    kv_ref[:, :D_KV] = kn.astype(BF16)
    kv_ref[:, D_KV:] = v.astype(BF16)
    p_ref[...] = y[:, D_ATTN + 2 * D_KV:]


def _in_proj(x2, g1, w_in, gq, gk):
    T, D = x2.shape
    d_in = w_in.shape[1]
    head_of = np.arange(D_ATTN) // HEAD_DIM
    indq = jnp.asarray(head_of[:, None] == head_of[None, :], BF16)
    indk = indq[:D_KV, :D_KV]
    full = lambda r, c: pl.BlockSpec((r, c), lambda i: (0, 0))
    rows = lambda c: pl.BlockSpec((IN_PROJ_ROWS, c), lambda i: (i, 0))
    return pl.pallas_call(
        _in_proj_kernel,
        grid=(T // IN_PROJ_ROWS,),
        in_specs=[rows(D), full(1, D), full(D, d_in), full(1, D_ATTN), full(1, D_KV),
                  full(D_ATTN, D_ATTN), full(D_KV, D_KV)],
        out_specs=[rows(D_ATTN), rows(2 * D_KV), rows(D_POOL)],
        out_shape=[jax.ShapeDtypeStruct((T, D_ATTN), BF16),
                   jax.ShapeDtypeStruct((T, 2 * D_KV), BF16),
                   jax.ShapeDtypeStruct((T, D_POOL), F32)],
        compiler_params=pltpu.CompilerParams(dimension_semantics=("arbitrary",),
                                             vmem_limit_bytes=VMEM_LIMIT),
        name="in_proj",
    )(x2, g1.reshape(1, D), w_in.astype(BF16),
      jnp.tile(gq, ATTN_HEADS).reshape(1, D_ATTN), jnp.tile(gk, ATTN_KV_HEADS).reshape(1, D_KV),
      indq, indk)


def _t5_bucket_table():
    i = np.arange(CHUNK)[:, None]
    j = np.arange(BAND)[None, :]
    rel = (j - WIN_CHUNKS * CHUNK) - i
    nb = REL_BUCKETS // 2
    max_exact = nb // 2
    base = np.where(rel > 0, nb, 0)
    n = np.abs(rel)
    nf = np.maximum(n, 1).astype(np.float64)
    large = max_exact + (np.log(nf / max_exact) / math.log(REL_MAX_DIST / max_exact)
                         * (nb - max_exact)).astype(np.int32)
    large = np.minimum(large, nb - 1)
    return (base + np.where(n < max_exact, n, large)).astype(np.int32)


def _mixer_kernel(x_ref, q_ref, kvc_ref, kvp_ref, pc_ref, pp_ref, bkt_ref, relb_ref, sink_ref,
                  poolw_ref, pscale_ref, wout_ref, g2_ref,
                  x1_ref, h2_ref,
                  bias_scr, kv_scr, p_scr, a_scr):
    b = pl.program_id(0)
    i = pl.program_id(1)
    rows = x_ref.shape[0]

    @pl.when((b == 0) & (i == 0))
    def _build_bias():
        bkt = bkt_ref[...]
        for h in range(ATTN_HEADS):
            acc = jnp.zeros((CHUNK, BAND), F32)
            for bucket in range(REL_BUCKETS):
                acc = jnp.where(bkt == bucket, relb_ref[bucket, h], acc)
            bias_scr[h] = acc

    kv_scr[:KV_HALO] = kvp_ref[...]
    kv_scr[KV_HALO:] = kvc_ref[...]
    p_scr[:POOL_HALO] = jnp.where(i == 0, 0.0, pp_ref[...])
    p_scr[POOL_HALO:] = pc_ref[...]

    key_slot = lax.broadcasted_iota(jnp.int32, (CHUNK, BAND), 1) // CHUNK
    for c in range(rows // CHUNK):
        first_key_chunk = i * (rows // CHUNK) + c - WIN_CHUNKS
        valid = (key_slot + first_key_chunk) >= 0
        for kvh in range(ATTN_KV_HEADS):
            kb = kv_scr[c * CHUNK:c * CHUNK + BAND, kvh * HEAD_DIM:(kvh + 1) * HEAD_DIM]
            vb = kv_scr[c * CHUNK:c * CHUNK + BAND, D_KV + kvh * HEAD_DIM:D_KV + (kvh + 1) * HEAD_DIM]
            for g in range(ATTN_GROUP):
                h = kvh * ATTN_GROUP + g
                qh = q_ref[c * CHUNK:(c + 1) * CHUNK, h * HEAD_DIM:(h + 1) * HEAD_DIM]
                s = lax.dot_general(qh, kb, (((1,), (1,)), ((), ())), preferred_element_type=F32)
                s = jnp.where(valid, s + bias_scr[h], NEG_INF)
                sink = sink_ref[h]
                m = jnp.maximum(jnp.max(s, axis=-1, keepdims=True), sink)
                e = jnp.exp(s - m)
                denom = jnp.sum(e, axis=-1, keepdims=True) + jnp.exp(sink - m)
                o = jnp.dot((e / denom).astype(BF16), vb, preferred_element_type=F32)
                a_scr[c * CHUNK:(c + 1) * CHUNK, h * HEAD_DIM:(h + 1) * HEAD_DIM] = o

    t_seq = i * rows + lax.broadcasted_iota(jnp.int32, (rows, 1), 0)
    pooled_out = []
    for g, w in enumerate(POOL_WINDOWS):
        lanes = slice(g * POOL_GROUP_DIM, (g + 1) * POOL_GROUP_DIM)
        cur = p_scr[POOL_HALO:POOL_HALO + rows, lanes]
        total = cur
        for j in range(1, w):
            total = total + p_scr[POOL_HALO - j:POOL_HALO - j + rows, lanes]
        cnt = jnp.minimum(t_seq + 1, w).astype(F32)
        d = total / cnt - cur
        og = jnp.dot(d.astype(BF16), poolw_ref[g], preferred_element_type=F32)
        pooled_out.append(og * pscale_ref[:, lanes])
    pooled = jnp.concatenate(pooled_out, axis=1)

    mixed = (jnp.dot(a_scr[...].astype(BF16), wout_ref[:D_ATTN], preferred_element_type=F32)
             + jnp.dot(pooled.astype(BF16), wout_ref[D_ATTN:], preferred_element_type=F32))
    x1 = x_ref[...] + mixed
    x1_ref[...] = x1
    h2_ref[...] = _rms(x1, g2_ref[...])


def _mixer(x2, q, kv, p, rel_bias, sinks, pool_w, pool_scale, w_out, g2, batch, seq):
    T, D = x2.shape
    nq = seq // MIXER_ROWS
    cur = lambda c: pl.BlockSpec((MIXER_ROWS, c), lambda b, i: (b * nq + i, 0))
    full2 = lambda r, c: pl.BlockSpec((r, c), lambda b, i: (0, 0))

    def prev(rows_per_block, c):
        per = MIXER_ROWS // rows_per_block
        return pl.BlockSpec((rows_per_block, c),
                            lambda b, i: (jnp.maximum((b * nq + i) * per - 1, 0), 0))

    smem = pl.BlockSpec(memory_space=pltpu.SMEM)
    return pl.pallas_call(
        _mixer_kernel,
        grid=(batch, nq),
        in_specs=[cur(D), cur(D_ATTN), cur(2 * D_KV), prev(KV_HALO, 2 * D_KV),
                  cur(D_POOL), prev(POOL_HALO, D_POOL),
                  full2(CHUNK, BAND), smem, smem,
                  pl.BlockSpec((len(POOL_WINDOWS), POOL_GROUP_DIM, POOL_GROUP_DIM),
                               lambda b, i: (0, 0, 0)),
                  full2(1, D_POOL), full2(D, D), full2(1, D)],
        out_specs=[cur(D), cur(D)],
        out_shape=[jax.ShapeDtypeStruct((T, D), F32), jax.ShapeDtypeStruct((T, D), F32)],
        scratch_shapes=[pltpu.VMEM((ATTN_HEADS, CHUNK, BAND), F32),
                        pltpu.VMEM((KV_HALO + MIXER_ROWS, 2 * D_KV), BF16),
                        pltpu.VMEM((POOL_HALO + MIXER_ROWS, D_POOL), F32),
                        pltpu.VMEM((MIXER_ROWS, D_ATTN), F32)],
        compiler_params=pltpu.CompilerParams(dimension_semantics=("arbitrary", "arbitrary"),
                                             vmem_limit_bytes=VMEM_LIMIT),
        name="mixer",
    )(x2, q, kv, kv, p, p, jnp.asarray(_t5_bucket_table()), rel_bias.astype(F32),
      sinks.astype(F32), pool_w.astype(BF16), pool_scale.reshape(1, D_POOL),
      w_out.astype(BF16), g2.reshape(1, D))


def _top_rounds(s, iota_f, store):
    big = float(s.shape[0])
    for r in range(PEER_TOPK):
        m = jnp.max(s, axis=0, keepdims=True)
        idx = jnp.min(jnp.where(s == m, iota_f, big), axis=0, keepdims=True)
        store(r, m, idx)
        s = jnp.where(iota_f == idx, -jnp.inf, s)


def _route_kernel(h2_ref, wq_ref, sk_ref, exp_ref, gate_ref, val_scr, idx_scr, best_scr, pos_scr):
    rows = h2_ref.shape[0]
    tiles = rows // LANES
    h2b = h2_ref[...].astype(BF16)
    iota_keys = lax.broadcasted_iota(jnp.int32, (PEER_KEYS, LANES), 0).astype(F32)

    def half_body(hp, carry):
        qhp = jnp.dot(h2b, wq_ref[hp], preferred_element_type=F32).astype(BF16)
        for t in range(tiles):
            sl = slice(t * LANES, (t + 1) * LANES)
            sc = lax.dot_general(sk_ref[hp], qhp[sl], (((1,), (1,)), ((), ())),
                                 preferred_element_type=F32)

            def store(r, m, idx, sl=sl):
                val_scr[hp, r:r + 1, sl] = m
                idx_scr[hp, r:r + 1, sl] = idx

            _top_rounds(sc, iota_keys, store)
        return carry

    lax.fori_loop(0, 2 * PEER_HEADS, half_body, 0)

    iota_cand = lax.broadcasted_iota(jnp.int32, (PEER_TOPK * PEER_TOPK, LANES), 0).astype(F32)

    def head_body(h, carry):
        for t in range(tiles):
            sl = slice(t * LANES, (t + 1) * LANES)
            v1 = val_scr[2 * h, :, sl]
            v2 = val_scr[2 * h + 1, :, sl]
            i1 = idx_scr[2 * h, :, sl]
            i2 = idx_scr[2 * h + 1, :, sl]
            cand = jnp.concatenate([v1[a:a + 1, :] + v2 for a in range(PEER_TOPK)], axis=0)

            def store(r, m, pos, sl=sl):
                best_scr[r:r + 1, sl] = m
                pos_scr[r:r + 1, sl] = pos

            _top_rounds(cand, iota_cand, store)
            pos = pos_scr[:, sl].astype(jnp.int32)
            pa = pos // PEER_TOPK
            pb = pos % PEER_TOPK
            e1 = jnp.zeros((PEER_TOPK, LANES), F32)
            e2 = jnp.zeros((PEER_TOPK, LANES), F32)
            for a in range(PEER_TOPK):
                e1 = e1 + jnp.where(pa == a, i1[a:a + 1, :], 0.0)
                e2 = e2 + jnp.where(pb == a, i2[a:a + 1, :], 0.0)
            best = best_scr[:, sl]
            ex = jnp.exp(best - best[0:1, :])
            slot0 = pl.multiple_of(h * PEER_TOPK, PEER_TOPK)
            exp_ref[pl.ds(slot0, PEER_TOPK), sl] = (e1 * PEER_KEYS + e2).astype(jnp.int32)
            gate_ref[pl.ds(slot0, PEER_TOPK), sl] = ex / jnp.sum(ex, axis=0, keepdims=True)
        return carry

    lax.fori_loop(0, PEER_HEADS, head_body, 0)


def _route(h2, peer_wq, peer_subkeys):
    T, D = h2.shape
    halves = 2 * PEER_HEADS
    wq = peer_wq.astype(BF16).reshape(D, halves, PEER_HALF).transpose(1, 0, 2)
    sk = peer_subkeys.astype(BF16).reshape(halves, PEER_KEYS, PEER_HALF)
    return pl.pallas_call(
        _route_kernel,
        grid=(T // ROUTE_ROWS,),
        in_specs=[pl.BlockSpec((ROUTE_ROWS, D), lambda i: (i, 0)),
                  pl.BlockSpec((halves, D, PEER_HALF), lambda i: (0, 0, 0)),
                  pl.BlockSpec((halves, PEER_KEYS, PEER_HALF), lambda i: (0, 0, 0))],
        out_specs=[pl.BlockSpec((PEER_SLOTS, ROUTE_ROWS), lambda i: (0, i)),
                   pl.BlockSpec((PEER_SLOTS, ROUTE_ROWS), lambda i: (0, i))],
        out_shape=[jax.ShapeDtypeStruct((PEER_SLOTS, T), jnp.int32),
                   jax.ShapeDtypeStruct((PEER_SLOTS, T), F32)],
        scratch_shapes=[pltpu.VMEM((halves, PEER_TOPK, ROUTE_ROWS), F32),
                        pltpu.VMEM((halves, PEER_TOPK, ROUTE_ROWS), F32),
                        pltpu.VMEM((PEER_TOPK, ROUTE_ROWS), F32),
                        pltpu.VMEM((PEER_TOPK, ROUTE_ROWS), F32)],
        compiler_params=pltpu.CompilerParams(dimension_semantics=("arbitrary",),
                                             vmem_limit_bytes=VMEM_LIMIT),
        name="route",
    )(h2, wq, sk)


def _gelu(x):
    return 0.5 * x * (1.0 + lax.erf(x * (2.0 ** -0.5)))


def _expert_kernel(idx_ref, gate_ref, h2_ref, x1_ref, tab_hbm, tab_tiles_hbm, out_ref, ring, sems):
    i = pl.program_id(0)
    d = h2_ref.shape[1]
    words = d // (2 * LANES)
    groups = EXPERT_BLOCK // EXPERT_SUB
    tok_tiles = PEER_SLOTS // SUBLANES
    half = PEER_SLOTS // 2

    def issue(group, slot, tt, k0, k1):
        base = (group * EXPERT_SUB + tt) * PEER_SLOTS
        for k in range(k0, k1):
            row = idx_ref[0, 0, base + k]
            dst = ring.at[slot, tt * tok_tiles + k // SUBLANES, :, k % SUBLANES]
            pltpu.make_async_copy(tab_hbm.at[row], dst, sems.at[slot]).start(priority=k % 2)

    def wait(slot):
        pltpu.make_async_copy(tab_tiles_hbm.at[pl.ds(0, ring.shape[1])], ring.at[slot],
                              sems.at[slot]).wait()

    @pl.when(i == 0)
    def _prologue():
        for a in range(EXPERT_AHEAD):
            for tt in range(EXPERT_SUB):
                issue(a, a, tt, 0, PEER_SLOTS)

    lane = lax.broadcasted_iota(jnp.int32, (PEER_SLOTS, LANES), 1)
    sub = lax.broadcasted_iota(jnp.int32, (EXPERT_SUB, LANES), 0)

    def group_body(g, slot):
        ahead = (slot + EXPERT_AHEAD) % EXPERT_RING
        wait(slot)
        tok_rows = pl.ds(pl.multiple_of(g * EXPERT_SUB, EXPERT_SUB), EXPERT_SUB)
        hsub = h2_ref[tok_rows, :]

        def tile_pair(tt, c):
            t0 = tt * tok_tiles
            w = ring[slot, t0:t0 + tok_tiles, c].reshape(PEER_SLOTS, LANES)
            lo = lax.bitcast_convert_type(w << 16, F32)
            hi = lax.bitcast_convert_type(w & jnp.uint32(0xFFFF0000), F32)
            return lo, hi

        def cols(j):
            return slice(j * LANES, (j + 1) * LANES)

        s = jnp.zeros((PEER_SLOTS, LANES), F32)
        for tt in range(EXPERT_SUB):
            issue(g + EXPERT_AHEAD, ahead, tt, 0, half)
            acc = None
            for c in range(words):
                lo, hi = tile_pair(tt, c)
                term = lo * hsub[tt:tt + 1, cols(2 * c)] + hi * hsub[tt:tt + 1, cols(2 * c + 1)]
                acc = term if acc is None else acc + term
            s = jnp.where(lane == tt, jnp.sum(acc, axis=1, keepdims=True), s)
        gates = pltpu.roll(gate_ref[...], (EXPERT_BLOCK - g * EXPERT_SUB) % EXPERT_BLOCK, 1)
        coef = gates * _gelu(s)
        o = [jnp.zeros((EXPERT_SUB, LANES), F32) for _ in range(2 * words)]
        for tt in range(EXPERT_SUB):
            issue(g + EXPERT_AHEAD, ahead, tt, half, PEER_SLOTS)
            cc = coef[:, tt:tt + 1]
            for c in range(words):
                for j, part in enumerate(tile_pair(tt, words + c)):
                    oc = jnp.sum(part * cc, axis=0, keepdims=True)
                    o[2 * c + j] = jnp.where(sub == tt, oc, o[2 * c + j])
        for j in range(2 * words):
            out_ref[tok_rows, cols(j)] = x1_ref[tok_rows, cols(j)] + o[j]

    def ring_turn(q, carry):
        for r in range(EXPERT_RING):
            group_body(q * EXPERT_RING + r, r)
        return carry

    lax.fori_loop(0, groups // EXPERT_RING, ring_turn, 0)

    @pl.when(i == pl.num_programs(0) - 1)
    def _drain():
        for a in range(EXPERT_AHEAD):
            wait((groups + a) % EXPERT_RING)


def _pack_rows(w):
    n, d = w.shape
    pairs = w.astype(BF16).reshape(n, d // (2 * LANES), 2, LANES).transpose(0, 1, 3, 2)
    return lax.bitcast_convert_type(pairs, jnp.uint32)


def _experts(experts_t, gates_t, h2, x1, peer_u, peer_v):
    T, D = h2.shape
    blocks = T // EXPERT_BLOCK
    sub_rows = EXPERT_SUB * PEER_SLOTS
    ahead = EXPERT_AHEAD * sub_rows
    row_tiles = D // LANES
    n_experts = peer_u.shape[0]
    assert (EXPERT_BLOCK // EXPERT_SUB) % EXPERT_RING == 0 and n_experts % SUBLANES == 0
    ids = experts_t.T.reshape(blocks, EXPERT_BLOCK * PEER_SLOTS)
    ids = jnp.concatenate([ids, jnp.roll(ids[:, :ahead], -1, axis=0).at[-1].set(ids[-1, :ahead])],
                          axis=1).reshape(blocks, 1, EXPERT_BLOCK * PEER_SLOTS + ahead)
    table = jnp.concatenate([_pack_rows(peer_u), _pack_rows(peer_v)], axis=1)
    table_tiles = table.reshape(n_experts // SUBLANES, row_tiles, SUBLANES, LANES)
    tok = lambda c: pl.BlockSpec((EXPERT_BLOCK, c), lambda i: (i, 0))
    return pl.pallas_call(
        _expert_kernel,
        grid=(blocks,),
        in_specs=[pl.BlockSpec((1, 1, EXPERT_BLOCK * PEER_SLOTS + ahead), lambda i: (i, 0, 0),
                               memory_space=pltpu.SMEM),
                  pl.BlockSpec((PEER_SLOTS, EXPERT_BLOCK), lambda i: (0, i)),
                  tok(D), tok(D),
                  pl.BlockSpec(memory_space=pl.ANY), pl.BlockSpec(memory_space=pl.ANY)],
        out_specs=tok(D),
        out_shape=jax.ShapeDtypeStruct((T, D), F32),
        scratch_shapes=[pltpu.VMEM((EXPERT_RING, sub_rows // SUBLANES, row_tiles, SUBLANES, LANES),
                                   jnp.uint32),
                        pltpu.SemaphoreType.DMA((EXPERT_RING,))],
        compiler_params=pltpu.CompilerParams(dimension_semantics=("arbitrary",),
                                             vmem_limit_bytes=VMEM_LIMIT),
        name="experts",
    )(ids, gates_t, h2, x1, table, table_tiles)


def kernel(x, norm1_g, w_in, q_norm_g, k_norm_g, attn_sinks, rel_bias, pool_w, pool_scale, w_out,
           norm2_g, peer_wq, peer_subkeys, peer_u, peer_v):
    batch, seq, d_model = x.shape
    depth = norm1_g.shape[0]
    assert seq % MIXER_ROWS == 0 and (batch * seq) % IN_PROJ_ROWS == 0
    assert (batch * seq) % ROUTE_ROWS == 0 and (batch * seq) % EXPERT_BLOCK == 0
    x2 = x.reshape(batch * seq, d_model)
    for l in range(depth):
        q, kv, p = _in_proj(x2, norm1_g[l], w_in[l], q_norm_g[l], k_norm_g[l])
        x1, h2 = _mixer(x2, q, kv, p, rel_bias, attn_sinks[l], pool_w[l], pool_scale[l],
                        w_out[l], norm2_g[l], batch, seq)
        experts_t, gates_t = _route(h2, peer_wq[l], peer_subkeys[l])
        x2 = _experts(experts_t, gates_t, h2, x1, peer_u[l], peer_v[l])
    return x2.reshape(batch, seq, d_model)
```

```python
import functools
import math

import jax
import jax.numpy as jnp
import numpy as np
from jax import lax
from jax.experimental import pallas as pl
from jax.experimental.pallas import tpu as pltpu

F32 = jnp.float32
BF16 = jnp.bfloat16

CHUNK = 64
ATTN_HEADS = 8
ATTN_KV_HEADS = 2
HEAD_DIM = 64
ATTN_GROUP = ATTN_HEADS // ATTN_KV_HEADS
WIN_CHUNKS = 2
BAND = (WIN_CHUNKS + 1) * CHUNK
D_ATTN = ATTN_HEADS * HEAD_DIM
D_KV = ATTN_KV_HEADS * HEAD_DIM
POOL_WINDOWS = (2, 4, 8, 16)
POOL_GROUP_DIM = 128
D_POOL = len(POOL_WINDOWS) * POOL_GROUP_DIM
REL_BUCKETS = 32
REL_MAX_DIST = 128
PEER_KEYS = 128
PEER_HEADS = 8
PEER_TOPK = 16
PEER_HALF = 128
PEER_SLOTS = PEER_HEADS * PEER_TOPK
EPS = 1e-6
NEG_INF = -1e30

LANES = 128
SUBLANES = 8

IN_PROJ_ROWS = 512
MIXER_ROWS = 256
POOL_HALO = 16
KV_HALO = WIN_CHUNKS * CHUNK
ROUTE_ROWS = 512
EXPERT_BLOCK = 128
EXPERT_SUB = 8
EXPERT_AHEAD = 3
EXPERT_RING = EXPERT_AHEAD + 1
VMEM_LIMIT = 48 * 1024 * 1024


def _rms(x, g):
    return x * lax.rsqrt(jnp.mean(x * x, axis=-1, keepdims=True) + EPS) * g


def _segment_sumsq(x, ind):
    sq = x * x
    hi = sq.astype(BF16)
    lo = (sq - hi.astype(F32)).astype(BF16)
    return (jnp.dot(hi, ind, preferred_element_type=F32)
            + jnp.dot(lo, ind, preferred_element_type=F32))


def _in_proj_kernel(x_ref, g1_ref, w_ref, gq_ref, gk_ref, indq_ref, indk_ref,
                    q_ref, kv_ref, p_ref):
    h = _rms(x_ref[...], g1_ref[...])
    y = jnp.dot(h.astype(BF16), w_ref[...], preferred_element_type=F32)
    q = y[:, :D_ATTN]
    k = y[:, D_ATTN:D_ATTN + D_KV]
    v = y[:, D_ATTN + D_KV:D_ATTN + 2 * D_KV]
    qn = q * lax.rsqrt(_segment_sumsq(q, indq_ref[...]) * (1.0 / HEAD_DIM) + EPS) * gq_ref[...]
    kn = k * lax.rsqrt(_segment_sumsq(k, indk_ref[...]) * (1.0 / HEAD_DIM) + EPS) * gk_ref[...]
    q_ref[...] = (qn * (HEAD_DIM ** -0.5)).astype(BF16)
    kv_ref[:, :D_KV] = kn.astype(BF16)
    kv_ref[:, D_KV:] = v.astype(BF16)
    p_ref[...] = y[:, D_ATTN + 2 * D_KV:]


def _in_proj(x2, g1, w_in, gq, gk):
    T, D = x2.shape
    d_in = w_in.shape[1]
    head_of = np.arange(D_ATTN) // HEAD_DIM
    indq = jnp.asarray(head_of[:, None] == head_of[None, :], BF16)
    indk = indq[:D_KV, :D_KV]
    full = lambda r, c: pl.BlockSpec((r, c), lambda i: (0, 0))
    rows = lambda c: pl.BlockSpec((IN_PROJ_ROWS, c), lambda i: (i, 0))
    return pl.pallas_call(
        _in_proj_kernel,
        grid=(T // IN_PROJ_ROWS,),
        in_specs=[rows(D), full(1, D), full(D, d_in), full(1, D_ATTN), full(1, D_KV),
                  full(D_ATTN, D_ATTN), full(D_KV, D_KV)],
        out_specs=[rows(D_ATTN), rows(2 * D_KV), rows(D_POOL)],
        out_shape=[jax.ShapeDtypeStruct((T, D_ATTN), BF16),
                   jax.ShapeDtypeStruct((T, 2 * D_KV), BF16),
                   jax.ShapeDtypeStruct((T, D_POOL), F32)],
        compiler_params=pltpu.CompilerParams(dimension_semantics=("arbitrary",),
                                             vmem_limit_bytes=VMEM_LIMIT),
        name="in_proj",
    )(x2, g1.reshape(1, D), w_in.astype(BF16),
      jnp.tile(gq, ATTN_HEADS).reshape(1, D_ATTN), jnp.tile(gk, ATTN_KV_HEADS).reshape(1, D_KV),
      indq, indk)


def _t5_bucket_table():
    i = np.arange(CHUNK)[:, None]
    j = np.arange(BAND)[None, :]
    rel = (j - WIN_CHUNKS * CHUNK) - i
    nb = REL_BUCKETS // 2
    max_exact = nb // 2
    base = np.where(rel > 0, nb, 0)
    n = np.abs(rel)
    nf = np.maximum(n, 1).astype(np.float64)
    large = max_exact + (np.log(nf / max_exact) / math.log(REL_MAX_DIST / max_exact)
                         * (nb - max_exact)).astype(np.int32)
    large = np.minimum(large, nb - 1)
    return (base + np.where(n < max_exact, n, large)).astype(np.int32)


def _mixer_kernel(x_ref, q_ref, kvc_ref, kvp_ref, pc_ref, pp_ref, bkt_ref, relb_ref, sink_ref,
                  poolw_ref, pscale_ref, wout_ref, g2_ref,
                  x1_ref, h2_ref,
                  bias_scr, kv_scr, p_scr):
    b = pl.program_id(0)
    i = pl.program_id(1)
    rows = x_ref.shape[0]

    @pl.when((b == 0) & (i == 0))
    def _build_bias():
        bkt = bkt_ref[...]
        for h in range(ATTN_HEADS):
            acc = jnp.zeros((CHUNK, BAND), F32)
            for bucket in range(REL_BUCKETS):
                acc = jnp.where(bkt == bucket, relb_ref[bucket, h], acc)
            bias_scr[h] = acc

    kv_scr[:KV_HALO] = kvp_ref[...]
    kv_scr[KV_HALO:] = kvc_ref[...]
    p_scr[:POOL_HALO] = jnp.where(i == 0, 0.0, pp_ref[...])
    p_scr[POOL_HALO:] = pc_ref[...]

    key_slot = lax.broadcasted_iota(jnp.int32, (CHUNK, BAND), 1) // CHUNK
    attn_chunks = []
    for c in range(rows // CHUNK):
        first_key_chunk = i * (rows // CHUNK) + c - WIN_CHUNKS
        valid = (key_slot + first_key_chunk) >= 0
        band = slice(c * CHUNK, c * CHUNK + BAND)
        scores = []
        for h in range(ATTN_HEADS):
            kvh = h // ATTN_GROUP
            qh = q_ref[c * CHUNK:(c + 1) * CHUNK, h * HEAD_DIM:(h + 1) * HEAD_DIM]
            kb = kv_scr[band, kvh * HEAD_DIM:(kvh + 1) * HEAD_DIM]
            s = lax.dot_general(qh, kb, (((1,), (1,)), ((), ())), preferred_element_type=F32)
            scores.append(jnp.where(valid, s + bias_scr[h], NEG_INF))
        probs = []
        for h in range(ATTN_HEADS):
            s = scores[h]
            sink = sink_ref[h]
            m = jnp.maximum(jnp.max(s, axis=-1, keepdims=True), sink)
            e = jnp.exp(s - m)
            denom = jnp.sum(e, axis=-1, keepdims=True) + jnp.exp(sink - m)
            probs.append((e / denom).astype(BF16))
        heads_out = []
        for h in range(ATTN_HEADS):
            kvh = h // ATTN_GROUP
            vb = kv_scr[band, D_KV + kvh * HEAD_DIM:D_KV + (kvh + 1) * HEAD_DIM]
            heads_out.append(jnp.dot(probs[h], vb, preferred_element_type=F32))
        attn_chunks.append(jnp.concatenate(heads_out, axis=1).astype(BF16))
    attn = jnp.concatenate(attn_chunks, axis=0)

    t_seq = i * rows + lax.broadcasted_iota(jnp.int32, (rows, 1), 0)
    pooled_out = []
    for g, w in enumerate(POOL_WINDOWS):
        lanes = slice(g * POOL_GROUP_DIM, (g + 1) * POOL_GROUP_DIM)
        cur = p_scr[POOL_HALO:POOL_HALO + rows, lanes]
        total = cur
        for j in range(1, w):
            total = total + p_scr[POOL_HALO - j:POOL_HALO - j + rows, lanes]
        cnt = jnp.minimum(t_seq + 1, w).astype(F32)
        d = total / cnt - cur
        og = jnp.dot(d.astype(BF16), poolw_ref[g], preferred_element_type=F32)
        pooled_out.append(og * pscale_ref[:, lanes])
    pooled = jnp.concatenate(pooled_out, axis=1)

    mixed = (jnp.dot(attn, wout_ref[:D_ATTN], preferred_element_type=F32)
             + jnp.dot(pooled.astype(BF16), wout_ref[D_ATTN:], preferred_element_type=F32))
    x1 = x_ref[...] + mixed
    x1_ref[...] = x1
    h2_ref[...] = _rms(x1, g2_ref[...])


def _mixer(x2, q, kv, p, rel_bias, sinks, pool_w, pool_scale, w_out, g2, batch, seq):
    T, D = x2.shape
    nq = seq // MIXER_ROWS
    cur = lambda c: pl.BlockSpec((MIXER_ROWS, c), lambda b, i: (b * nq + i, 0))
    full2 = lambda r, c: pl.BlockSpec((r, c), lambda b, i: (0, 0))

    def prev(rows_per_block, c):
        per = MIXER_ROWS // rows_per_block
        return pl.BlockSpec((rows_per_block, c),
                            lambda b, i: (jnp.maximum((b * nq + i) * per - 1, 0), 0))

    smem = pl.BlockSpec(memory_space=pltpu.SMEM)
    return pl.pallas_call(
        _mixer_kernel,
        grid=(batch, nq),
        in_specs=[cur(D), cur(D_ATTN), cur(2 * D_KV), prev(KV_HALO, 2 * D_KV),
                  cur(D_POOL), prev(POOL_HALO, D_POOL),
                  full2(CHUNK, BAND), smem, smem,
                  pl.BlockSpec((len(POOL_WINDOWS), POOL_GROUP_DIM, POOL_GROUP_DIM),
                               lambda b, i: (0, 0, 0)),
                  full2(1, D_POOL), full2(D, D), full2(1, D)],
        out_specs=[cur(D), cur(D)],
        out_shape=[jax.ShapeDtypeStruct((T, D), F32), jax.ShapeDtypeStruct((T, D), F32)],
        scratch_shapes=[pltpu.VMEM((ATTN_HEADS, CHUNK, BAND), F32),
                        pltpu.VMEM((KV_HALO + MIXER_ROWS, 2 * D_KV), BF16),
                        pltpu.VMEM((POOL_HALO + MIXER_ROWS, D_POOL), F32)],
        compiler_params=pltpu.CompilerParams(dimension_semantics=("arbitrary", "arbitrary"),
                                             vmem_limit_bytes=VMEM_LIMIT),
        name="mixer",
    )(x2, q, kv, kv, p, p, jnp.asarray(_t5_bucket_table()), rel_bias.astype(F32),
      sinks.astype(F32), pool_w.astype(BF16), pool_scale.reshape(1, D_POOL),
      w_out.astype(BF16), g2.reshape(1, D))


def _top_rounds(s, ident, store):
    big = 1e9
    for r in range(PEER_TOPK):
        m = jnp.max(s, axis=0, keepdims=True)
        idx = jnp.min(jnp.where(s == m, ident, big), axis=0, keepdims=True)
        store(r, m, idx)
        s = jnp.where(ident == idx, -jnp.inf, s)


def _candidate_tables():
    pos, pen = [], []
    for a, width in [(0, PEER_TOPK)] + [(a, SUBLANES) for a in range(1, SUBLANES)]:
        for b in range(width):
            pos.append(a * PEER_TOPK + b)
            pen.append(0.0 if (a + 1) * (b + 1) <= PEER_TOPK else -np.inf)
    for a in range(SUBLANES, PEER_TOPK):
        pos.append(a * PEER_TOPK)
        pen.append(0.0)
    return (np.asarray(pos, np.float32).reshape(-1, 1), np.asarray(pen, np.float32).reshape(-1, 1))


def _route_kernel(h2_ref, wq_ref, sk_ref, pos_ref, pen_ref, exp_ref, gate_ref,
                  val_scr, idx_scr, best_scr, win_scr):
    rows = h2_ref.shape[0]
    tiles = rows // LANES
    h2b = h2_ref[...].astype(BF16)
    iota_keys = lax.broadcasted_iota(jnp.int32, (PEER_KEYS, LANES), 0).astype(F32)
    cand_pos = jnp.broadcast_to(pos_ref[...], (pos_ref.shape[0], LANES))
    cand_pen = jnp.broadcast_to(pen_ref[...], (pen_ref.shape[0], LANES))

    def head_body(h, carry):
        qh = jnp.dot(h2b, wq_ref[h], preferred_element_type=F32)
        for p in range(2):
            qhp = qh[:, p * PEER_HALF:(p + 1) * PEER_HALF].astype(BF16)
            for t in range(tiles):
                sl = slice(t * LANES, (t + 1) * LANES)
                sc = lax.dot_general(sk_ref[2 * h + p], qhp[sl], (((1,), (1,)), ((), ())),
                                     preferred_element_type=F32)

                def store(r, m, idx, p=p, sl=sl):
                    val_scr[p, r:r + 1, sl] = m
                    idx_scr[p, r:r + 1, sl] = idx

                _top_rounds(sc, iota_keys, store)
        for t in range(tiles):
            sl = slice(t * LANES, (t + 1) * LANES)
            v1 = val_scr[0, :, sl]
            v2 = val_scr[1, :, sl]
            i1 = idx_scr[0, :, sl]
            i2 = idx_scr[1, :, sl]
            cand = jnp.concatenate(
                [v1[0:1, :] + v2]
                + [v1[a:a + 1, :] + v2[0:SUBLANES, :] for a in range(1, SUBLANES)]
                + [v1[SUBLANES:, :] + v2[0:1, :]], axis=0) + cand_pen

            def store(r, m, pos, sl=sl):
                best_scr[r:r + 1, sl] = m
                win_scr[r:r + 1, sl] = pos

            _top_rounds(cand, cand_pos, store)
            pos = win_scr[:, sl].astype(jnp.int32)
            pa = pos // PEER_TOPK
            pb = pos % PEER_TOPK
            e1 = jnp.zeros((PEER_TOPK, LANES), F32)
            e2 = jnp.zeros((PEER_TOPK, LANES), F32)
            for a in range(PEER_TOPK):
                e1 = e1 + jnp.where(pa == a, i1[a:a + 1, :], 0.0)
                e2 = e2 + jnp.where(pb == a, i2[a:a + 1, :], 0.0)
            best = best_scr[:, sl]
            ex = jnp.exp(best - best[0:1, :])
            slot0 = pl.multiple_of(h * PEER_TOPK, PEER_TOPK)
            exp_ref[pl.ds(slot0, PEER_TOPK), sl] = (e1 * PEER_KEYS + e2).astype(jnp.int32)
            gate_ref[pl.ds(slot0, PEER_TOPK), sl] = ex / jnp.sum(ex, axis=0, keepdims=True)
        return carry

    lax.fori_loop(0, PEER_HEADS, head_body, 0)


def _route(h2, peer_wq, peer_subkeys):
    T, D = h2.shape
    wq = peer_wq.astype(BF16).reshape(D, PEER_HEADS, 2 * PEER_HALF).transpose(1, 0, 2)
    sk = peer_subkeys.astype(BF16).reshape(2 * PEER_HEADS, PEER_KEYS, PEER_HALF)
    pos, pen = _candidate_tables()
    n_cand = pos.shape[0]
    return pl.pallas_call(
        _route_kernel,
        grid=(T // ROUTE_ROWS,),
        in_specs=[pl.BlockSpec((ROUTE_ROWS, D), lambda i: (i, 0)),
                  pl.BlockSpec((PEER_HEADS, D, 2 * PEER_HALF), lambda i: (0, 0, 0)),
                  pl.BlockSpec((2 * PEER_HEADS, PEER_KEYS, PEER_HALF), lambda i: (0, 0, 0)),
                  pl.BlockSpec((n_cand, 1), lambda i: (0, 0)),
                  pl.BlockSpec((n_cand, 1), lambda i: (0, 0))],
        out_specs=[pl.BlockSpec((PEER_SLOTS, ROUTE_ROWS), lambda i: (0, i)),
                   pl.BlockSpec((PEER_SLOTS, ROUTE_ROWS), lambda i: (0, i))],
        out_shape=[jax.ShapeDtypeStruct((PEER_SLOTS, T), jnp.int32),
                   jax.ShapeDtypeStruct((PEER_SLOTS, T), F32)],
        scratch_shapes=[pltpu.VMEM((2, PEER_TOPK, ROUTE_ROWS), F32),
                        pltpu.VMEM((2, PEER_TOPK, ROUTE_ROWS), F32),
                        pltpu.VMEM((PEER_TOPK, ROUTE_ROWS), F32),
                        pltpu.VMEM((PEER_TOPK, ROUTE_ROWS), F32)],
        compiler_params=pltpu.CompilerParams(dimension_semantics=("arbitrary",),
                                             vmem_limit_bytes=VMEM_LIMIT),
        name="route",
    )(h2, wq, sk, jnp.asarray(pos), jnp.asarray(pen))


def _gelu(x):
    return 0.5 * x * (1.0 + lax.erf(x * (2.0 ** -0.5)))


def _expert_kernel(idx_ref, gate_ref, h2_ref, x1_ref, tab_hbm, tab_tiles_hbm, out_ref, ring, sems):
    i = pl.program_id(0)
    d = h2_ref.shape[1]
    words = d // (2 * LANES)
    groups = EXPERT_BLOCK // EXPERT_SUB
    tok_tiles = PEER_SLOTS // SUBLANES
    half = PEER_SLOTS // 2

    def issue(group, slot, tt, k0, k1):
        base = (group * EXPERT_SUB + tt) * PEER_SLOTS
        for k in range(k0, k1):
            row = idx_ref[0, 0, base + k]
            dst = ring.at[slot, tt * tok_tiles + k // SUBLANES, :, k % SUBLANES]
            pltpu.make_async_copy(tab_hbm.at[row], dst, sems.at[slot]).start(priority=k % 2)

    def wait(slot):
        pltpu.make_async_copy(tab_tiles_hbm.at[pl.ds(0, ring.shape[1])], ring.at[slot],
                              sems.at[slot]).wait()

    @pl.when(i == 0)
    def _prologue():
        for a in range(EXPERT_AHEAD):
            for tt in range(EXPERT_SUB):
                issue(a, a, tt, 0, PEER_SLOTS)

    lane = lax.broadcasted_iota(jnp.int32, (PEER_SLOTS, LANES), 1)
    sub = lax.broadcasted_iota(jnp.int32, (EXPERT_SUB, LANES), 0)

    def group_body(g, slot):
        ahead = (slot + EXPERT_AHEAD) % EXPERT_RING
        wait(slot)
        tok_rows = pl.ds(pl.multiple_of(g * EXPERT_SUB, EXPERT_SUB), EXPERT_SUB)
        hsub = h2_ref[tok_rows, :]

        def tile_pair(tt, c):
            t0 = tt * tok_tiles
            w = ring[slot, t0:t0 + tok_tiles, c].reshape(PEER_SLOTS, LANES)
            lo = lax.bitcast_convert_type(w << 16, F32)
            hi = lax.bitcast_convert_type(w & jnp.uint32(0xFFFF0000), F32)
            return lo, hi

        def cols(j):
            return slice(j * LANES, (j + 1) * LANES)

        s = jnp.zeros((PEER_SLOTS, LANES), F32)
        for tt in range(EXPERT_SUB):
            issue(g + EXPERT_AHEAD, ahead, tt, 0, half)
            acc = None
            for c in range(words):
                lo, hi = tile_pair(tt, c)
                term = lo * hsub[tt:tt + 1, cols(2 * c)] + hi * hsub[tt:tt + 1, cols(2 * c + 1)]
                acc = term if acc is None else acc + term
            s = jnp.where(lane == tt, jnp.sum(acc, axis=1, keepdims=True), s)
        gates = pltpu.roll(gate_ref[...], (EXPERT_BLOCK - g * EXPERT_SUB) % EXPERT_BLOCK, 1)
        coef = gates * _gelu(s)
        o = [jnp.zeros((EXPERT_SUB, LANES), F32) for _ in range(2 * words)]
        for tt in range(EXPERT_SUB):
            issue(g + EXPERT_AHEAD, ahead, tt, half, PEER_SLOTS)
            cc = coef[:, tt:tt + 1]
            for c in range(words):
                for j, part in enumerate(tile_pair(tt, words + c)):
                    oc = jnp.sum(part * cc, axis=0, keepdims=True)
                    o[2 * c + j] = jnp.where(sub == tt, oc, o[2 * c + j])
        for j in range(2 * words):
            out_ref[tok_rows, cols(j)] = x1_ref[tok_rows, cols(j)] + o[j]

    def ring_turn(q, carry):
        for r in range(EXPERT_RING):
            group_body(q * EXPERT_RING + r, r)
        return carry

    lax.fori_loop(0, groups // EXPERT_RING, ring_turn, 0)

    @pl.when(i == pl.num_programs(0) - 1)
    def _drain():
        for a in range(EXPERT_AHEAD):
            wait((groups + a) % EXPERT_RING)


def _pack_rows(w):
    n, d = w.shape
    pairs = w.astype(BF16).reshape(n, d // (2 * LANES), 2, LANES).transpose(0, 1, 3, 2)
    return lax.bitcast_convert_type(pairs, jnp.uint32)


def _experts(experts_t, gates_t, h2, x1, peer_u, peer_v):
    T, D = h2.shape
    blocks = T // EXPERT_BLOCK
    sub_rows = EXPERT_SUB * PEER_SLOTS
    ahead = EXPERT_AHEAD * sub_rows
    row_tiles = D // LANES
    n_experts = peer_u.shape[0]
    assert (EXPERT_BLOCK // EXPERT_SUB) % EXPERT_RING == 0 and n_experts % SUBLANES == 0
    ids = experts_t.T.reshape(blocks, EXPERT_BLOCK * PEER_SLOTS)
    ids = jnp.concatenate([ids, jnp.roll(ids[:, :ahead], -1, axis=0).at[-1].set(ids[-1, :ahead])],
                          axis=1).reshape(blocks, 1, EXPERT_BLOCK * PEER_SLOTS + ahead)
    table = jnp.concatenate([_pack_rows(peer_u), _pack_rows(peer_v)], axis=1)
    table_tiles = table.reshape(n_experts // SUBLANES, row_tiles, SUBLANES, LANES)
    tok = lambda c: pl.BlockSpec((EXPERT_BLOCK, c), lambda i: (i, 0))
    return pl.pallas_call(
        _expert_kernel,
        grid=(blocks,),
        in_specs=[pl.BlockSpec((1, 1, EXPERT_BLOCK * PEER_SLOTS + ahead), lambda i: (i, 0, 0),
                               memory_space=pltpu.SMEM),
                  pl.BlockSpec((PEER_SLOTS, EXPERT_BLOCK), lambda i: (0, i)),
                  tok(D), tok(D),
                  pl.BlockSpec(memory_space=pl.ANY), pl.BlockSpec(memory_space=pl.ANY)],
        out_specs=tok(D),
        out_shape=jax.ShapeDtypeStruct((T, D), F32),
        scratch_shapes=[pltpu.VMEM((EXPERT_RING, sub_rows // SUBLANES, row_tiles, SUBLANES, LANES),
                                   jnp.uint32),
                        pltpu.SemaphoreType.DMA((EXPERT_RING,))],
        compiler_params=pltpu.CompilerParams(dimension_semantics=("arbitrary",),
                                             vmem_limit_bytes=VMEM_LIMIT),
        name="experts",
    )(ids, gates_t, h2, x1, table, table_tiles)


def kernel(x, norm1_g, w_in, q_norm_g, k_norm_g, attn_sinks, rel_bias, pool_w, pool_scale, w_out,
           norm2_g, peer_wq, peer_subkeys, peer_u, peer_v):
    batch, seq, d_model = x.shape
    depth = norm1_g.shape[0]
    assert seq % MIXER_ROWS == 0 and (batch * seq) % IN_PROJ_ROWS == 0
    assert (batch * seq) % ROUTE_ROWS == 0 and (batch * seq) % EXPERT_BLOCK == 0
    x2 = x.reshape(batch * seq, d_model)
    for l in range(depth):
        q, kv, p = _in_proj(x2, norm1_g[l], w_in[l], q_norm_g[l], k_norm_g[l])
        x1, h2 = _mixer(x2, q, kv, p, rel_bias, attn_sinks[l], pool_w[l], pool_scale[l],
                        w_out[l], norm2_g[l], batch, seq)
        experts_t, gates_t = _route(h2, peer_wq[l], peer_subkeys[l])
        x2 = _experts(experts_t, gates_t, h2, x1, peer_u[l], peer_v[l])
    return x2.reshape(batch, seq, d_model)
```

```python
import math

import jax
import jax.numpy as jnp
import numpy as np
from jax import lax
from jax.experimental import pallas as pl
from jax.experimental.pallas import tpu as pltpu

F32 = jnp.float32
BF16 = jnp.bfloat16

CHUNK = 64
ATTN_HEADS = 8
ATTN_KV_HEADS = 2
HEAD_DIM = 64
ATTN_GROUP = ATTN_HEADS // ATTN_KV_HEADS
WIN_CHUNKS = 2
BAND = (WIN_CHUNKS + 1) * CHUNK
D_ATTN = ATTN_HEADS * HEAD_DIM
D_KV = ATTN_KV_HEADS * HEAD_DIM
POOL_WINDOWS = (2, 4, 8, 16)
POOL_GROUP_DIM = 128
D_POOL = len(POOL_WINDOWS) * POOL_GROUP_DIM
REL_BUCKETS = 32
REL_MAX_DIST = 128
PEER_KEYS = 128
PEER_HEADS = 8
PEER_TOPK = 16
PEER_HALF = 128
PEER_SLOTS = PEER_HEADS * PEER_TOPK
EPS = 1e-6
NEG_INF = -1e30

LANES = 128
SUBLANES = 8

IN_PROJ_ROWS = 512
MIXER_ROWS = 256
POOL_HALO = 16
KV_HALO = WIN_CHUNKS * CHUNK
PACK_ROWS = 256
EXPERT_BLOCK = 128
EXPERT_SUB = 8
EXPERT_AHEAD = 3
EXPERT_RING = EXPERT_AHEAD + 1
ROUTE_AHEAD = 2
VMEM_LIMIT = 56 * 1024 * 1024


def _rms(x, g):
    return x * lax.rsqrt(jnp.mean(x * x, axis=-1, keepdims=True) + EPS) * g


def _segment_sumsq(x, ind):
    sq = x * x
    hi = sq.astype(BF16)
    lo = (sq - hi.astype(F32)).astype(BF16)
    return (jnp.dot(hi, ind, preferred_element_type=F32)
            + jnp.dot(lo, ind, preferred_element_type=F32))


def _in_proj_kernel(x_ref, g1_ref, w_ref, gq_ref, gk_ref, indq_ref, indk_ref,
                    q_ref, kv_ref, p_ref):
    h = _rms(x_ref[...], g1_ref[...])
    y = jnp.dot(h.astype(BF16), w_ref[...], preferred_element_type=F32)
    q = y[:, :D_ATTN]
    k = y[:, D_ATTN:D_ATTN + D_KV]
    v = y[:, D_ATTN + D_KV:D_ATTN + 2 * D_KV]
    qn = q * lax.rsqrt(_segment_sumsq(q, indq_ref[...]) * (1.0 / HEAD_DIM) + EPS) * gq_ref[...]
    kn = k * lax.rsqrt(_segment_sumsq(k, indk_ref[...]) * (1.0 / HEAD_DIM) + EPS) * gk_ref[...]
    q_ref[...] = (qn * (HEAD_DIM ** -0.5)).astype(BF16)
    kv_ref[:, :D_KV] = kn.astype(BF16)
    kv_ref[:, D_KV:] = v.astype(BF16)
    p_ref[...] = y[:, D_ATTN + 2 * D_KV:]


def _in_proj(x2, g1, w_in, gq, gk):
    T, D = x2.shape
    d_in = w_in.shape[1]
    head_of = np.arange(D_ATTN) // HEAD_DIM
    indq = jnp.asarray(head_of[:, None] == head_of[None, :], BF16)
    indk = indq[:D_KV, :D_KV]
    full = lambda r, c: pl.BlockSpec((r, c), lambda i: (0, 0))
    rows = lambda c: pl.BlockSpec((IN_PROJ_ROWS, c), lambda i: (i, 0))
    return pl.pallas_call(
        _in_proj_kernel,
        grid=(T // IN_PROJ_ROWS,),
        in_specs=[rows(D), full(1, D), full(D, d_in), full(1, D_ATTN), full(1, D_KV),
                  full(D_ATTN, D_ATTN), full(D_KV, D_KV)],
        out_specs=[rows(D_ATTN), rows(2 * D_KV), rows(D_POOL)],
        out_shape=[jax.ShapeDtypeStruct((T, D_ATTN), BF16),
                   jax.ShapeDtypeStruct((T, 2 * D_KV), BF16),
                   jax.ShapeDtypeStruct((T, D_POOL), F32)],
        compiler_params=pltpu.CompilerParams(dimension_semantics=("arbitrary",),
                                             vmem_limit_bytes=VMEM_LIMIT),
        name="in_proj",
    )(x2, g1.reshape(1, D), w_in.astype(BF16),
      jnp.tile(gq, ATTN_HEADS).reshape(1, D_ATTN), jnp.tile(gk, ATTN_KV_HEADS).reshape(1, D_KV),
      indq, indk)


def _t5_bucket_table():
    i = np.arange(CHUNK)[:, None]
    j = np.arange(BAND)[None, :]
    rel = (j - WIN_CHUNKS * CHUNK) - i
    nb = REL_BUCKETS // 2
    max_exact = nb // 2
    base = np.where(rel > 0, nb, 0)
    n = np.abs(rel)
    nf = np.maximum(n, 1).astype(np.float64)
    large = max_exact + (np.log(nf / max_exact) / math.log(REL_MAX_DIST / max_exact)
                         * (nb - max_exact)).astype(np.int32)
    large = np.minimum(large, nb - 1)
    return (base + np.where(n < max_exact, n, large)).astype(np.int32)


def _mixer_kernel(x_ref, q_ref, kvc_ref, kvp_ref, pc_ref, pp_ref, bkt_ref, relb_ref, sink_ref,
                  poolw_ref, pscale_ref, wout_ref, g2_ref,
                  x1_ref, h2_ref,
                  bias_scr, kv_scr, p_scr):
    b = pl.program_id(0)
    i = pl.program_id(1)
    rows = x_ref.shape[0]

    @pl.when((b == 0) & (i == 0))
    def _build_bias():
        bkt = bkt_ref[...]
        for h in range(ATTN_HEADS):
            acc = jnp.zeros((CHUNK, BAND), F32)
            for bucket in range(REL_BUCKETS):
                acc = jnp.where(bkt == bucket, relb_ref[bucket, h], acc)
            bias_scr[h] = acc

    kv_scr[:KV_HALO] = kvp_ref[...]
    kv_scr[KV_HALO:] = kvc_ref[...]
    p_scr[:POOL_HALO] = jnp.where(i == 0, 0.0, pp_ref[...])
    p_scr[POOL_HALO:] = pc_ref[...]

    key_slot = lax.broadcasted_iota(jnp.int32, (CHUNK, BAND), 1) // CHUNK
    attn_chunks = []
    for c in range(rows // CHUNK):
        first_key_chunk = i * (rows // CHUNK) + c - WIN_CHUNKS
        valid = (key_slot + first_key_chunk) >= 0
        band = slice(c * CHUNK, c * CHUNK + BAND)
        scores = []
        for h in range(ATTN_HEADS):
            kvh = h // ATTN_GROUP
            qh = q_ref[c * CHUNK:(c + 1) * CHUNK, h * HEAD_DIM:(h + 1) * HEAD_DIM]
            kb = kv_scr[band, kvh * HEAD_DIM:(kvh + 1) * HEAD_DIM]
            s = lax.dot_general(qh, kb, (((1,), (1,)), ((), ())), preferred_element_type=F32)
            scores.append(jnp.where(valid, s + bias_scr[h], NEG_INF))
        probs = []
        for h in range(ATTN_HEADS):
            s = scores[h]
            sink = sink_ref[h]
            m = jnp.maximum(jnp.max(s, axis=-1, keepdims=True), sink)
            e = jnp.exp(s - m)
            denom = jnp.sum(e, axis=-1, keepdims=True) + jnp.exp(sink - m)
            probs.append((e / denom).astype(BF16))
        heads_out = []
        for h in range(ATTN_HEADS):
            kvh = h // ATTN_GROUP
            vb = kv_scr[band, D_KV + kvh * HEAD_DIM:D_KV + (kvh + 1) * HEAD_DIM]
            heads_out.append(jnp.dot(probs[h], vb, preferred_element_type=F32))
        attn_chunks.append(jnp.concatenate(heads_out, axis=1).astype(BF16))
    attn = jnp.concatenate(attn_chunks, axis=0)

    t_seq = i * rows + lax.broadcasted_iota(jnp.int32, (rows, 1), 0)
    pooled_out = []
    for g, w in enumerate(POOL_WINDOWS):
        lanes = slice(g * POOL_GROUP_DIM, (g + 1) * POOL_GROUP_DIM)
        cur = p_scr[POOL_HALO:POOL_HALO + rows, lanes]
        total = cur
        for j in range(1, w):
            total = total + p_scr[POOL_HALO - j:POOL_HALO - j + rows, lanes]
        cnt = jnp.minimum(t_seq + 1, w).astype(F32)
        d = total / cnt - cur
        og = jnp.dot(d.astype(BF16), poolw_ref[g], preferred_element_type=F32)
        pooled_out.append(og * pscale_ref[:, lanes])
    pooled = jnp.concatenate(pooled_out, axis=1)

    mixed = (jnp.dot(attn, wout_ref[:D_ATTN], preferred_element_type=F32)
             + jnp.dot(pooled.astype(BF16), wout_ref[D_ATTN:], preferred_element_type=F32))
    x1 = x_ref[...] + mixed
    x1_ref[...] = x1
    h2_ref[...] = _rms(x1, g2_ref[...])


def _mixer(x2, q, kv, p, rel_bias, sinks, pool_w, pool_scale, w_out, g2, batch, seq):
    T, D = x2.shape
    nq = seq // MIXER_ROWS
    cur = lambda c: pl.BlockSpec((MIXER_ROWS, c), lambda b, i: (b * nq + i, 0))
    full2 = lambda r, c: pl.BlockSpec((r, c), lambda b, i: (0, 0))

    def prev(rows_per_block, c):
        per = MIXER_ROWS // rows_per_block
        return pl.BlockSpec((rows_per_block, c),
                            lambda b, i: (jnp.maximum((b * nq + i) * per - 1, 0), 0))

    smem = pl.BlockSpec(memory_space=pltpu.SMEM)
    return pl.pallas_call(
        _mixer_kernel,
        grid=(batch, nq),
        in_specs=[cur(D), cur(D_ATTN), cur(2 * D_KV), prev(KV_HALO, 2 * D_KV),
                  cur(D_POOL), prev(POOL_HALO, D_POOL),
                  full2(CHUNK, BAND), smem, smem,
                  pl.BlockSpec((len(POOL_WINDOWS), POOL_GROUP_DIM, POOL_GROUP_DIM),
                               lambda b, i: (0, 0, 0)),
                  full2(1, D_POOL), full2(D, D), full2(1, D)],
        out_specs=[cur(D), cur(D)],
        out_shape=[jax.ShapeDtypeStruct((T, D), F32), jax.ShapeDtypeStruct((T, D), F32)],
        scratch_shapes=[pltpu.VMEM((ATTN_HEADS, CHUNK, BAND), F32),
                        pltpu.VMEM((KV_HALO + MIXER_ROWS, 2 * D_KV), BF16),
                        pltpu.VMEM((POOL_HALO + MIXER_ROWS, D_POOL), F32)],
        compiler_params=pltpu.CompilerParams(dimension_semantics=("arbitrary", "arbitrary"),
                                             vmem_limit_bytes=VMEM_LIMIT),
        name="mixer",
    )(x2, q, kv, kv, p, p, jnp.asarray(_t5_bucket_table()), rel_bias.astype(F32),
      sinks.astype(F32), pool_w.astype(BF16), pool_scale.reshape(1, D_POOL),
      w_out.astype(BF16), g2.reshape(1, D))


def _pack_kernel(u_ref, v_ref, out_ref):
    rows = u_ref.shape[0]
    tiles = u_ref.shape[1] // LANES
    for t, src in enumerate((u_ref, v_ref)):
        for c in range(tiles):
            out_ref[pl.ds(t * tiles + c, rows, stride=2 * tiles), :] = src[:, c * LANES:(c + 1) * LANES]


def _pack_table(peer_u, peer_v):
    n, d = peer_u.shape
    row_tiles = 2 * d // LANES
    packed = pl.pallas_call(
        _pack_kernel,
        grid=(n // PACK_ROWS,),
        in_specs=[pl.BlockSpec((PACK_ROWS, d), lambda i: (i, 0)),
                  pl.BlockSpec((PACK_ROWS, d), lambda i: (i, 0))],
        out_specs=pl.BlockSpec((PACK_ROWS * row_tiles, LANES), lambda i: (i, 0)),
        out_shape=jax.ShapeDtypeStruct((n * row_tiles, LANES), F32),
        compiler_params=pltpu.CompilerParams(dimension_semantics=("arbitrary",),
                                             vmem_limit_bytes=VMEM_LIMIT),
        name="pack_table",
    )(peer_u, peer_v)
    return packed.reshape(n, row_tiles, LANES)


def _gelu(x):
    return 0.5 * x * (1.0 + lax.erf(x * (2.0 ** -0.5)))


def _top_rounds(s, ident, store):
    big = 1e9
    for r in range(PEER_TOPK):
        m = jnp.max(s, axis=0, keepdims=True)
        idx = jnp.min(jnp.where(s == m, ident, big), axis=0, keepdims=True)
        store(r, m, idx)
        s = jnp.where(ident == idx, -jnp.inf, s)


def _candidate_tables():
    pos, pen = [], []
    for a, width in [(0, PEER_TOPK)] + [(a, SUBLANES) for a in range(1, SUBLANES)]:
        for b in range(width):
            pos.append(a * PEER_TOPK + b)
            pen.append(0.0 if (a + 1) * (b + 1) <= PEER_TOPK else -np.inf)
    for a in range(SUBLANES, PEER_TOPK):
        pos.append(a * PEER_TOPK)
        pen.append(0.0)
    return (np.asarray(pos, np.float32).reshape(-1, 1), np.asarray(pen, np.float32).reshape(-1, 1))


def _route_half(hr_ref, wq_ref, sk_ref, h, p, val_scr, idx_scr):
    iota_keys = lax.broadcasted_iota(jnp.int32, (PEER_KEYS, LANES), 0).astype(F32)
    q = jnp.dot(hr_ref[...].astype(BF16), wq_ref[h, :, p * PEER_HALF:(p + 1) * PEER_HALF],
                preferred_element_type=F32).astype(BF16)
    sc = lax.dot_general(sk_ref[2 * h + p], q, (((1,), (1,)), ((), ())),
                         preferred_element_type=F32)

    def store(r, m, idx):
        val_scr[p, r:r + 1, :] = m
        idx_scr[p, r:r + 1, :] = idx

    _top_rounds(sc, iota_keys, store)


def _route_head(h, pos_ref, pen_ref, val_scr, idx_scr, best_scr, win_scr, ids_ref, gates_ref):
    cand_pos = jnp.broadcast_to(pos_ref[...], (pos_ref.shape[0], LANES))
    cand_pen = jnp.broadcast_to(pen_ref[...], (pen_ref.shape[0], LANES))
    v1 = val_scr[0]
    v2 = val_scr[1]
    i1 = idx_scr[0]
    i2 = idx_scr[1]
    cand = jnp.concatenate(
        [v1[0:1, :] + v2]
        + [v1[a:a + 1, :] + v2[0:SUBLANES, :] for a in range(1, SUBLANES)]
        + [v1[SUBLANES:, :] + v2[0:1, :]], axis=0) + cand_pen

    def store(r, m, pos):
        best_scr[r:r + 1, :] = m
        win_scr[r:r + 1, :] = pos

    _top_rounds(cand, cand_pos, store)
    pos = win_scr[...].astype(jnp.int32)
    pa = pos // PEER_TOPK
    pb = pos % PEER_TOPK
    e1 = jnp.zeros((PEER_TOPK, LANES), F32)
    e2 = jnp.zeros((PEER_TOPK, LANES), F32)
    for a in range(PEER_TOPK):
        e1 = e1 + jnp.where(pa == a, i1[a:a + 1, :], 0.0)
        e2 = e2 + jnp.where(pb == a, i2[a:a + 1, :], 0.0)
    best = best_scr[...]
    ex = jnp.exp(best - best[0:1, :])
    head_rows = pl.ds(pl.multiple_of(h * PEER_TOPK, PEER_TOPK), PEER_TOPK)
    ids_ref[head_rows, :] = (e1 * PEER_KEYS + e2).astype(jnp.int32)
    gates_ref[head_rows, :] = ex / jnp.sum(ex, axis=0, keepdims=True)


def _peer_kernel(h2_ref, h2r_ref, x1_ref, h2_hbm, wq_ref, sk_ref, pos_ref, pen_ref,
                 tab_hbm, tab_tiles_hbm, out_ref,
                 ring, ring_sems, ids_smem, ids_stage, gates_ring, hr_scr,
                 val_scr, idx_scr, best_scr, win_scr, aux_sems):
    i = pl.program_id(0)
    last = pl.num_programs(0) - 1
    d = h2_ref.shape[1]
    chunks = d // LANES
    groups = EXPERT_BLOCK // EXPERT_SUB
    tok_tiles = PEER_SLOTS // SUBLANES
    half = PEER_SLOTS // 2
    gate_slots = gates_ring.shape[0]

    def route_piece(hr_ref, h, p, gate_slot):
        _route_half(hr_ref, wq_ref, sk_ref, h, p, val_scr, idx_scr)
        if p == 1:
            _route_head(h, pos_ref, pen_ref, val_scr, idx_scr, best_scr, win_scr,
                        ids_stage, gates_ring.at[gate_slot])

    def publish_ids(parity):
        copy = pltpu.make_async_copy(ids_stage, ids_smem.at[parity], aux_sems.at[0])
        copy.start()
        copy.wait()

    def issue(group, slot, tt, k0, k1):
        if isinstance(group, int):
            block_off, in_block = divmod(group, groups)
        else:
            block_off, in_block = group >> (groups.bit_length() - 1), group & (groups - 1)
        parity = (i + block_off) & 1
        tok = in_block * EXPERT_SUB + tt
        for k in range(k0, k1):
            row = ids_smem[parity, k, tok]
            dst = ring.at[slot, tt * tok_tiles + k // SUBLANES, :, k % SUBLANES]
            pltpu.make_async_copy(tab_hbm.at[row], dst, ring_sems.at[slot]).start(priority=k % 2)

    def wait(slot):
        pltpu.make_async_copy(tab_tiles_hbm.at[pl.ds(0, ring.shape[1])], ring.at[slot],
                              ring_sems.at[slot]).wait()

    @pl.when(i == 0)
    def _prologue():
        for blk in range(ROUTE_AHEAD):
            if blk == 0:
                src = h2_ref
            else:
                copy = pltpu.make_async_copy(h2_hbm.at[pl.ds(blk * EXPERT_BLOCK, EXPERT_BLOCK)],
                                             hr_scr, aux_sems.at[1])
                copy.start()
                copy.wait()
                src = hr_scr

            def head(h, carry, src=src, blk=blk):
                route_piece(src, h, 0, blk)
                route_piece(src, h, 1, blk)
                return carry

            lax.fori_loop(0, PEER_HEADS, head, 0)
            publish_ids(blk)
        for a in range(EXPERT_AHEAD):
            for tt in range(EXPERT_SUB):
                issue(a, a, tt, 0, PEER_SLOTS)

    lane = lax.broadcasted_iota(jnp.int32, (PEER_SLOTS, LANES), 1)
    sub = lax.broadcasted_iota(jnp.int32, (EXPERT_SUB, LANES), 0)

    def group_body(g, slot):
        ahead = (slot + EXPERT_AHEAD) % EXPERT_RING
        wait(slot)
        tok_rows = pl.ds(pl.multiple_of(g * EXPERT_SUB, EXPERT_SUB), EXPERT_SUB)
        hsub = h2_ref[tok_rows, :]

        def tile(tt, c):
            t0 = tt * tok_tiles
            return ring[slot, t0:t0 + tok_tiles, c].reshape(PEER_SLOTS, LANES)

        def cols(j):
            return slice(j * LANES, (j + 1) * LANES)

        s = jnp.zeros((PEER_SLOTS, LANES), F32)
        for tt in range(EXPERT_SUB):
            issue(g + EXPERT_AHEAD, ahead, tt, 0, half)
            acc = tile(tt, 0) * hsub[tt:tt + 1, cols(0)]
            for c in range(1, chunks):
                acc = acc + tile(tt, c) * hsub[tt:tt + 1, cols(c)]
            s = jnp.where(lane == tt, jnp.sum(acc, axis=1, keepdims=True), s)
        gates = pltpu.roll(gates_ring[i % gate_slots],
                           (EXPERT_BLOCK - g * EXPERT_SUB) % EXPERT_BLOCK, 1)
        coef = gates * _gelu(s)
        o = [jnp.zeros((EXPERT_SUB, LANES), F32) for _ in range(chunks)]
        for tt in range(EXPERT_SUB):
            issue(g + EXPERT_AHEAD, ahead, tt, half, PEER_SLOTS)
            cc = coef[:, tt:tt + 1]
            for c in range(chunks):
                oc = jnp.sum(tile(tt, chunks + c) * cc, axis=0, keepdims=True)
                o[c] = jnp.where(sub == tt, oc, o[c])
        for c in range(chunks):
            out_ref[tok_rows, cols(c)] = x1_ref[tok_rows, cols(c)] + o[c]

    routed_gate_slot = (i + ROUTE_AHEAD) % gate_slots

    def ring_turn(q, carry):
        for r in range(EXPERT_RING):
            group_body(q * EXPERT_RING + r, r)
            route_piece(h2r_ref, q * (EXPERT_RING // 2) + r // 2, r % 2, routed_gate_slot)
        return carry

    lax.fori_loop(0, groups // EXPERT_RING, ring_turn, 0)
    publish_ids((i + ROUTE_AHEAD) % 2)

    @pl.when(i == last)
    def _drain():
        for a in range(EXPERT_AHEAD):
            wait((groups + a) % EXPERT_RING)


def _peer(h2, x1, peer_wq, peer_subkeys, peer_u, peer_v):
    T, D = h2.shape
    blocks = T // EXPERT_BLOCK
    groups = EXPERT_BLOCK // EXPERT_SUB
    sub_rows = EXPERT_SUB * PEER_SLOTS
    row_tiles = 2 * D // LANES
    n_experts = peer_u.shape[0]
    assert groups % EXPERT_RING == 0 and groups == 2 * PEER_HEADS and EXPERT_RING % 2 == 0
    assert groups & (groups - 1) == 0
    assert ROUTE_AHEAD == 2 and blocks >= ROUTE_AHEAD and n_experts % SUBLANES == 0
    assert EXPERT_BLOCK == LANES
    wq = peer_wq.astype(BF16).reshape(D, PEER_HEADS, 2 * PEER_HALF).transpose(1, 0, 2)
    sk = peer_subkeys.astype(BF16).reshape(2 * PEER_HEADS, PEER_KEYS, PEER_HALF)
    pos, pen = _candidate_tables()
    n_cand = pos.shape[0]
    table = _pack_table(peer_u, peer_v)
    table_tiles = table.reshape(n_experts // SUBLANES, row_tiles, SUBLANES, LANES)
    tok = lambda c: pl.BlockSpec((EXPERT_BLOCK, c), lambda i: (i, 0))
    const = lambda *shape: pl.BlockSpec(shape, lambda i: (0,) * len(shape))
    hbm = pl.BlockSpec(memory_space=pl.ANY)
    return pl.pallas_call(
        _peer_kernel,
        grid=(blocks,),
        in_specs=[tok(D),
                  pl.BlockSpec((EXPERT_BLOCK, D),
                               lambda i: (jnp.minimum(i + ROUTE_AHEAD, blocks - 1), 0)),
                  tok(D), hbm,
                  const(PEER_HEADS, D, 2 * PEER_HALF), const(2 * PEER_HEADS, PEER_KEYS, PEER_HALF),
                  const(n_cand, 1), const(n_cand, 1),
                  hbm, hbm],
        out_specs=tok(D),
        out_shape=jax.ShapeDtypeStruct((T, D), F32),
        scratch_shapes=[pltpu.VMEM((EXPERT_RING, sub_rows // SUBLANES, row_tiles, SUBLANES, LANES),
                                   F32),
                        pltpu.SemaphoreType.DMA((EXPERT_RING,)),
                        pltpu.SMEM((2, PEER_SLOTS, EXPERT_BLOCK), jnp.int32),
                        pltpu.VMEM((PEER_SLOTS, EXPERT_BLOCK), jnp.int32),
                        pltpu.VMEM((ROUTE_AHEAD + 1, PEER_SLOTS, EXPERT_BLOCK), F32),
                        pltpu.VMEM((EXPERT_BLOCK, D), F32),
                        pltpu.VMEM((2, PEER_TOPK, EXPERT_BLOCK), F32),
                        pltpu.VMEM((2, PEER_TOPK, EXPERT_BLOCK), F32),
                        pltpu.VMEM((PEER_TOPK, EXPERT_BLOCK), F32),
                        pltpu.VMEM((PEER_TOPK, EXPERT_BLOCK), F32),
                        pltpu.SemaphoreType.DMA((2,))],
        compiler_params=pltpu.CompilerParams(dimension_semantics=("arbitrary",),
                                             vmem_limit_bytes=VMEM_LIMIT),
        name="peer",
    )(h2, h2, x1, h2, wq, sk, jnp.asarray(pos), jnp.asarray(pen), table, table_tiles)


def kernel(x, norm1_g, w_in, q_norm_g, k_norm_g, attn_sinks, rel_bias, pool_w, pool_scale, w_out,
           norm2_g, peer_wq, peer_subkeys, peer_u, peer_v):
    batch, seq, d_model = x.shape
    depth = norm1_g.shape[0]
    assert seq % MIXER_ROWS == 0 and (batch * seq) % IN_PROJ_ROWS == 0
    assert (batch * seq) % EXPERT_BLOCK == 0 and peer_u.shape[1] % PACK_ROWS == 0
    x2 = x.reshape(batch * seq, d_model)
    for l in range(depth):
        q, kv, p = _in_proj(x2, norm1_g[l], w_in[l], q_norm_g[l], k_norm_g[l])
        x1, h2 = _mixer(x2, q, kv, p, rel_bias, attn_sinks[l], pool_w[l], pool_scale[l],
                        w_out[l], norm2_g[l], batch, seq)
        x2 = _peer(h2, x1, peer_wq[l], peer_subkeys[l], peer_u[l], peer_v[l])
    return x2.reshape(batch, seq, d_model)
```

```python
import math

import jax
import jax.numpy as jnp
import numpy as np
from jax import lax
from jax.experimental import pallas as pl
from jax.experimental.pallas import tpu as pltpu

F32 = jnp.float32
BF16 = jnp.bfloat16

CHUNK = 64
ATTN_HEADS = 8
ATTN_KV_HEADS = 2
HEAD_DIM = 64
ATTN_GROUP = ATTN_HEADS // ATTN_KV_HEADS
WIN_CHUNKS = 2
BAND = (WIN_CHUNKS + 1) * CHUNK
D_ATTN = ATTN_HEADS * HEAD_DIM
D_KV = ATTN_KV_HEADS * HEAD_DIM
POOL_WINDOWS = (2, 4, 8, 16)
POOL_GROUP_DIM = 128
D_POOL = len(POOL_WINDOWS) * POOL_GROUP_DIM
REL_BUCKETS = 32
REL_MAX_DIST = 128
PEER_KEYS = 128
PEER_HEADS = 8
PEER_TOPK = 16
PEER_HALF = 128
PEER_SLOTS = PEER_HEADS * PEER_TOPK
EPS = 1e-6
NEG_INF = -1e30

LANES = 128
SUBLANES = 8

IN_PROJ_ROWS = 512
MIXER_ROWS = 256
POOL_HALO = 16
KV_HALO = WIN_CHUNKS * CHUNK
PACK_ROWS = 256
EXPERT_BLOCK = 128
EXPERT_SUB = 8
EXPERT_AHEAD = 3
EXPERT_RING = EXPERT_AHEAD + 1
ROUTE_AHEAD = 2
VMEM_LIMIT = 56 * 1024 * 1024


def _rms(x, g):
    return x * lax.rsqrt(jnp.mean(x * x, axis=-1, keepdims=True) + EPS) * g


def _segment_sumsq(x, ind):
    sq = x * x
    hi = sq.astype(BF16)
    lo = (sq - hi.astype(F32)).astype(BF16)
    return (jnp.dot(hi, ind, preferred_element_type=F32)
            + jnp.dot(lo, ind, preferred_element_type=F32))


def _in_proj_kernel(x_ref, g1_ref, w_ref, gq_ref, gk_ref, indq_ref, indk_ref,
                    q_ref, kv_ref, p_ref):
    h = _rms(x_ref[...], g1_ref[...])
    y = jnp.dot(h.astype(BF16), w_ref[...], preferred_element_type=F32)
    q = y[:, :D_ATTN]
    k = y[:, D_ATTN:D_ATTN + D_KV]
    v = y[:, D_ATTN + D_KV:D_ATTN + 2 * D_KV]
    qn = q * lax.rsqrt(_segment_sumsq(q, indq_ref[...]) * (1.0 / HEAD_DIM) + EPS) * gq_ref[...]
    kn = k * lax.rsqrt(_segment_sumsq(k, indk_ref[...]) * (1.0 / HEAD_DIM) + EPS) * gk_ref[...]
    q_ref[...] = (qn * (HEAD_DIM ** -0.5)).astype(BF16)
    kv_ref[:, :D_KV] = kn.astype(BF16)
    kv_ref[:, D_KV:] = v.astype(BF16)
    p_ref[...] = y[:, D_ATTN + 2 * D_KV:]


def _in_proj(x2, g1, w_in, gq, gk):
    T, D = x2.shape
    d_in = w_in.shape[1]
    head_of = np.arange(D_ATTN) // HEAD_DIM
    indq = jnp.asarray(head_of[:, None] == head_of[None, :], BF16)
    indk = indq[:D_KV, :D_KV]
    full = lambda r, c: pl.BlockSpec((r, c), lambda i: (0, 0))
    rows = lambda c: pl.BlockSpec((IN_PROJ_ROWS, c), lambda i: (i, 0))
    return pl.pallas_call(
        _in_proj_kernel,
        grid=(T // IN_PROJ_ROWS,),
        in_specs=[rows(D), full(1, D), full(D, d_in), full(1, D_ATTN), full(1, D_KV),
                  full(D_ATTN, D_ATTN), full(D_KV, D_KV)],
        out_specs=[rows(D_ATTN), rows(2 * D_KV), rows(D_POOL)],
        out_shape=[jax.ShapeDtypeStruct((T, D_ATTN), BF16),
                   jax.ShapeDtypeStruct((T, 2 * D_KV), BF16),
                   jax.ShapeDtypeStruct((T, D_POOL), F32)],
        compiler_params=pltpu.CompilerParams(dimension_semantics=("arbitrary",),
                                             vmem_limit_bytes=VMEM_LIMIT),
        name="in_proj",
    )(x2, g1.reshape(1, D), w_in.astype(BF16),
      jnp.tile(gq, ATTN_HEADS).reshape(1, D_ATTN), jnp.tile(gk, ATTN_KV_HEADS).reshape(1, D_KV),
      indq, indk)


def _t5_bucket_table():
    i = np.arange(CHUNK)[:, None]
    j = np.arange(BAND)[None, :]
    rel = (j - WIN_CHUNKS * CHUNK) - i
    nb = REL_BUCKETS // 2
    max_exact = nb // 2
    base = np.where(rel > 0, nb, 0)
    n = np.abs(rel)
    nf = np.maximum(n, 1).astype(np.float64)
    large = max_exact + (np.log(nf / max_exact) / math.log(REL_MAX_DIST / max_exact)
                         * (nb - max_exact)).astype(np.int32)
    large = np.minimum(large, nb - 1)
    return (base + np.where(n < max_exact, n, large)).astype(np.int32)


def _mixer_kernel(x_ref, q_ref, kvc_ref, kvp_ref, pc_ref, pp_ref, bkt_ref, relb_ref, sink_ref,
                  poolw_ref, pscale_ref, wout_ref, g2_ref,
                  x1_ref, h2_ref,
                  bias_scr, kv_scr, p_scr):
    b = pl.program_id(0)
    i = pl.program_id(1)
    rows = x_ref.shape[0]

    @pl.when((b == 0) & (i == 0))
    def _build_bias():
        bkt = bkt_ref[...]
        for h in range(ATTN_HEADS):
            acc = jnp.zeros((CHUNK, BAND), F32)
            for bucket in range(REL_BUCKETS):
                acc = jnp.where(bkt == bucket, relb_ref[bucket, h], acc)
            bias_scr[h] = acc

    kv_scr[:KV_HALO] = kvp_ref[...]
    kv_scr[KV_HALO:] = kvc_ref[...]
    p_scr[:POOL_HALO] = jnp.where(i == 0, 0.0, pp_ref[...])
    p_scr[POOL_HALO:] = pc_ref[...]

    key_slot = lax.broadcasted_iota(jnp.int32, (CHUNK, BAND), 1) // CHUNK
    attn_chunks = []
    for c in range(rows // CHUNK):
        first_key_chunk = i * (rows // CHUNK) + c - WIN_CHUNKS
        valid = (key_slot + first_key_chunk) >= 0
        band = slice(c * CHUNK, c * CHUNK + BAND)
        scores = []
        for h in range(ATTN_HEADS):
            kvh = h // ATTN_GROUP
            qh = q_ref[c * CHUNK:(c + 1) * CHUNK, h * HEAD_DIM:(h + 1) * HEAD_DIM]
            kb = kv_scr[band, kvh * HEAD_DIM:(kvh + 1) * HEAD_DIM]
            s = lax.dot_general(qh, kb, (((1,), (1,)), ((), ())), preferred_element_type=F32)
            scores.append(jnp.where(valid, s + bias_scr[h], NEG_INF))
        probs = []
        for h in range(ATTN_HEADS):
            s = scores[h]
            sink = sink_ref[h]
            m = jnp.maximum(jnp.max(s, axis=-1, keepdims=True), sink)
            e = jnp.exp(s - m)
            denom = jnp.sum(e, axis=-1, keepdims=True) + jnp.exp(sink - m)
            probs.append((e / denom).astype(BF16))
        heads_out = []
        for h in range(ATTN_HEADS):
            kvh = h // ATTN_GROUP
            vb = kv_scr[band, D_KV + kvh * HEAD_DIM:D_KV + (kvh + 1) * HEAD_DIM]
            heads_out.append(jnp.dot(probs[h], vb, preferred_element_type=F32))
        attn_chunks.append(jnp.concatenate(heads_out, axis=1).astype(BF16))
    attn = jnp.concatenate(attn_chunks, axis=0)

    t_seq = i * rows + lax.broadcasted_iota(jnp.int32, (rows, 1), 0)
    pooled_out = []
    for g, w in enumerate(POOL_WINDOWS):
        lanes = slice(g * POOL_GROUP_DIM, (g + 1) * POOL_GROUP_DIM)
        cur = p_scr[POOL_HALO:POOL_HALO + rows, lanes]
        total = cur
        for j in range(1, w):
            total = total + p_scr[POOL_HALO - j:POOL_HALO - j + rows, lanes]
        cnt = jnp.minimum(t_seq + 1, w).astype(F32)
        d = total / cnt - cur
        og = jnp.dot(d.astype(BF16), poolw_ref[g], preferred_element_type=F32)
        pooled_out.append(og * pscale_ref[:, lanes])
    pooled = jnp.concatenate(pooled_out, axis=1)

    mixed = (jnp.dot(attn, wout_ref[:D_ATTN], preferred_element_type=F32)
             + jnp.dot(pooled.astype(BF16), wout_ref[D_ATTN:], preferred_element_type=F32))
    x1 = x_ref[...] + mixed
    x1_ref[...] = x1
    h2_ref[...] = _rms(x1, g2_ref[...])


def _mixer(x2, q, kv, p, rel_bias, sinks, pool_w, pool_scale, w_out, g2, batch, seq):
    T, D = x2.shape
    nq = seq // MIXER_ROWS
    cur = lambda c: pl.BlockSpec((MIXER_ROWS, c), lambda b, i: (b * nq + i, 0))
    full2 = lambda r, c: pl.BlockSpec((r, c), lambda b, i: (0, 0))

    def prev(rows_per_block, c):
        per = MIXER_ROWS // rows_per_block
        return pl.BlockSpec((rows_per_block, c),
                            lambda b, i: (jnp.maximum((b * nq + i) * per - 1, 0), 0))

    smem = pl.BlockSpec(memory_space=pltpu.SMEM)
    return pl.pallas_call(
        _mixer_kernel,
        grid=(batch, nq),
        in_specs=[cur(D), cur(D_ATTN), cur(2 * D_KV), prev(KV_HALO, 2 * D_KV),
                  cur(D_POOL), prev(POOL_HALO, D_POOL),
                  full2(CHUNK, BAND), smem, smem,
                  pl.BlockSpec((len(POOL_WINDOWS), POOL_GROUP_DIM, POOL_GROUP_DIM),
                               lambda b, i: (0, 0, 0)),
                  full2(1, D_POOL), full2(D, D), full2(1, D)],
        out_specs=[cur(D), cur(D)],
        out_shape=[jax.ShapeDtypeStruct((T, D), F32), jax.ShapeDtypeStruct((T, D), F32)],
        scratch_shapes=[pltpu.VMEM((ATTN_HEADS, CHUNK, BAND), F32),
                        pltpu.VMEM((KV_HALO + MIXER_ROWS, 2 * D_KV), BF16),
                        pltpu.VMEM((POOL_HALO + MIXER_ROWS, D_POOL), F32)],
        compiler_params=pltpu.CompilerParams(dimension_semantics=("arbitrary", "arbitrary"),
                                             vmem_limit_bytes=VMEM_LIMIT),
        name="mixer",
    )(x2, q, kv, kv, p, p, jnp.asarray(_t5_bucket_table()), rel_bias.astype(F32),
      sinks.astype(F32), pool_w.astype(BF16), pool_scale.reshape(1, D_POOL),
      w_out.astype(BF16), g2.reshape(1, D))


def _pack_kernel(u_ref, v_ref, out_ref):
    rows = u_ref.shape[0]
    tiles = u_ref.shape[1] // LANES
    for t, src in enumerate((u_ref, v_ref)):
        for c in range(tiles):
            out_ref[pl.ds(t * tiles + c, rows, stride=2 * tiles), :] = src[:, c * LANES:(c + 1) * LANES]


def _pack_table(peer_u, peer_v):
    n, d = peer_u.shape
    row_tiles = 2 * d // LANES
    packed = pl.pallas_call(
        _pack_kernel,
        grid=(n // PACK_ROWS,),
        in_specs=[pl.BlockSpec((PACK_ROWS, d), lambda i: (i, 0)),
                  pl.BlockSpec((PACK_ROWS, d), lambda i: (i, 0))],
        out_specs=pl.BlockSpec((PACK_ROWS * row_tiles, LANES), lambda i: (i, 0)),
        out_shape=jax.ShapeDtypeStruct((n * row_tiles, LANES), F32),
        compiler_params=pltpu.CompilerParams(dimension_semantics=("arbitrary",),
                                             vmem_limit_bytes=VMEM_LIMIT),
        name="pack_table",
    )(peer_u, peer_v)
    return packed.reshape(n, row_tiles, LANES)


def _gelu(x):
    return 0.5 * x * (1.0 + lax.erf(x * (2.0 ** -0.5)))


def _top_round_steps(state, scores, ident, values, winners):
    row = lax.broadcasted_iota(jnp.int32, (PEER_TOPK, LANES), 0)

    def make(r):
        def step():
            s, ids = state[scores], state[ident]
            m = jnp.max(s, axis=0, keepdims=True)
            idx = jnp.min(jnp.where(s == m, ids, 1e9), axis=0, keepdims=True)
            state[values] = jnp.where(row == r, m, state[values])
            state[winners] = jnp.where(row == r, idx, state[winners])
            state[scores] = jnp.where(ids == idx, -jnp.inf, s)
        return step

    return [make(r) for r in range(PEER_TOPK)]


def _candidate_tables():
    pos, pen = [], []
    for a, width in [(0, PEER_TOPK)] + [(a, SUBLANES) for a in range(1, SUBLANES)]:
        for b in range(width):
            pos.append(a * PEER_TOPK + b)
            pen.append(0.0 if (a + 1) * (b + 1) <= PEER_TOPK else -np.inf)
    for a in range(SUBLANES, PEER_TOPK):
        pos.append(a * PEER_TOPK)
        pen.append(0.0)
    return (np.asarray(pos, np.float32).reshape(-1, 1), np.asarray(pen, np.float32).reshape(-1, 1))


def _route_steps(hr_ref, wq_ref, sk_ref, pos_ref, pen_ref, h, p, val_scr, idx_scr, ids_ref, gates_ref):
    st = {}
    zeros = jnp.zeros((PEER_TOPK, LANES), F32)

    def start():
        q = jnp.dot(hr_ref[...].astype(BF16), wq_ref[h, :, p * PEER_HALF:(p + 1) * PEER_HALF],
                    preferred_element_type=F32).astype(BF16)
        st["scores"] = lax.dot_general(sk_ref[2 * h + p], q, (((1,), (1,)), ((), ())),
                                       preferred_element_type=F32)
        st["keys"] = lax.broadcasted_iota(jnp.int32, (PEER_KEYS, LANES), 0).astype(F32)
        st["val"], st["idx"] = zeros, zeros

    steps = [start] + _top_round_steps(st, "scores", "keys", "val", "idx")

    if p == 0:
        def finish():
            val_scr[...] = st["val"]
            idx_scr[...] = st["idx"]

        return steps + [finish]

    def start_head():
        v1, v2 = val_scr[...], st["val"]
        pen = jnp.broadcast_to(pen_ref[...], (pen_ref.shape[0], LANES))
        st["cand"] = jnp.concatenate(
            [v1[0:1, :] + v2]
            + [v1[a:a + 1, :] + v2[0:SUBLANES, :] for a in range(1, SUBLANES)]
            + [v1[SUBLANES:, :] + v2[0:1, :]], axis=0) + pen
        st["pos"] = jnp.broadcast_to(pos_ref[...], (pos_ref.shape[0], LANES))
        st["best"], st["win"] = zeros, zeros

    def finish_head():
        i1, i2 = idx_scr[...], st["idx"]
        pos = st["win"].astype(jnp.int32)
        pa = pos // PEER_TOPK
        pb = pos % PEER_TOPK
        e1, e2 = zeros, zeros
        for a in range(PEER_TOPK):
            e1 = e1 + jnp.where(pa == a, i1[a:a + 1, :], 0.0)
            e2 = e2 + jnp.where(pb == a, i2[a:a + 1, :], 0.0)
        best = st["best"]
        ex = jnp.exp(best - best[0:1, :])
        head_rows = pl.ds(pl.multiple_of(h * PEER_TOPK, PEER_TOPK), PEER_TOPK)
        ids_ref[head_rows, :] = (e1 * PEER_KEYS + e2).astype(jnp.int32)
        gates_ref[head_rows, :] = ex / jnp.sum(ex, axis=0, keepdims=True)

    return (steps + [start_head] + _top_round_steps(st, "cand", "pos", "best", "win")
            + [finish_head])


def _peer_kernel(h2_ref, h2r_ref, x1_ref, h2_hbm, wq_ref, sk_ref, pos_ref, pen_ref,
                 tab_hbm, tab_tiles_hbm, out_ref,
                 ring, ring_sems, ids_smem, ids_stage, gates_ring, hr_scr,
                 val_scr, idx_scr, aux_sems):
    i = pl.program_id(0)
    last = pl.num_programs(0) - 1
    d = h2_ref.shape[1]
    chunks = d // LANES
    groups = EXPERT_BLOCK // EXPERT_SUB
    tok_tiles = PEER_SLOTS // SUBLANES
    half = PEER_SLOTS // 2
    gate_slots = gates_ring.shape[0]

    def route_steps(hr_ref, h, p, gate_slot):
        return _route_steps(hr_ref, wq_ref, sk_ref, pos_ref, pen_ref, h, p, val_scr, idx_scr,
                            ids_stage, gates_ring.at[gate_slot])

    def publish_ids(parity):
        col0 = parity * EXPERT_BLOCK
        if not isinstance(col0, int):
            col0 = pl.multiple_of(col0, EXPERT_BLOCK)
        copy = pltpu.make_async_copy(ids_stage, ids_smem.at[:, pl.ds(col0, EXPERT_BLOCK)],
                                     aux_sems.at[0])
        copy.start()
        copy.wait()

    def issue(group, slot, tt, k0, k1):
        if isinstance(group, int):
            block_off, in_block = divmod(group, groups)
        else:
            block_off, in_block = group >> (groups.bit_length() - 1), group & (groups - 1)
        col = ((i + block_off) & 1) * EXPERT_BLOCK + in_block * EXPERT_SUB + tt
        for k in range(k0, k1):
            row = ids_smem[k, col]
            dst = ring.at[slot, tt * tok_tiles + k // SUBLANES, :, k % SUBLANES]
            pltpu.make_async_copy(tab_hbm.at[row], dst, ring_sems.at[slot]).start(priority=k % 2)

    def wait(slot):
        pltpu.make_async_copy(tab_tiles_hbm.at[pl.ds(0, ring.shape[1])], ring.at[slot],
                              ring_sems.at[slot]).wait()

    @pl.when(i == 0)
    def _prologue():
        for blk in range(ROUTE_AHEAD):
            if blk == 0:
                src = h2_ref
            else:
                copy = pltpu.make_async_copy(h2_hbm.at[pl.ds(blk * EXPERT_BLOCK, EXPERT_BLOCK)],
                                             hr_scr, aux_sems.at[1])
                copy.start()
                copy.wait()
                src = hr_scr

            def head(h, carry, src=src, blk=blk):
                for p in range(2):
                    for step in route_steps(src, h, p, blk):
                        step()
                return carry

            lax.fori_loop(0, PEER_HEADS, head, 0)
            publish_ids(blk)
        for a in range(EXPERT_AHEAD):
            for tt in range(EXPERT_SUB):
                issue(a, a, tt, 0, PEER_SLOTS)

    lane = lax.broadcasted_iota(jnp.int32, (PEER_SLOTS, LANES), 1)
    sub = lax.broadcasted_iota(jnp.int32, (EXPERT_SUB, LANES), 0)

    def group_body(g, slot, side_steps):
        ahead = (slot + EXPERT_AHEAD) % EXPERT_RING
        per_issue = -(-len(side_steps) // (2 * EXPERT_SUB))
        pending = list(side_steps)

        def run_side(n):
            for _ in range(min(n, len(pending))):
                pending.pop(0)()

        wait(slot)
        tok_rows = pl.ds(pl.multiple_of(g * EXPERT_SUB, EXPERT_SUB), EXPERT_SUB)
        hsub = h2_ref[tok_rows, :]

        def tile(tt, c):
            t0 = tt * tok_tiles
            return ring[slot, t0:t0 + tok_tiles, c].reshape(PEER_SLOTS, LANES)

        def cols(j):
            return slice(j * LANES, (j + 1) * LANES)

        s = jnp.zeros((PEER_SLOTS, LANES), F32)
        for tt in range(EXPERT_SUB):
            issue(g + EXPERT_AHEAD, ahead, tt, 0, half)
            run_side(per_issue)
            acc = tile(tt, 0) * hsub[tt:tt + 1, cols(0)]
            for c in range(1, chunks):
                acc = acc + tile(tt, c) * hsub[tt:tt + 1, cols(c)]
            s = jnp.where(lane == tt, jnp.sum(acc, axis=1, keepdims=True), s)
        gates = pltpu.roll(gates_ring[i % gate_slots],
                           (EXPERT_BLOCK - g * EXPERT_SUB) % EXPERT_BLOCK, 1)
        coef = gates * _gelu(s)
        o = [jnp.zeros((EXPERT_SUB, LANES), F32) for _ in range(chunks)]
        for tt in range(EXPERT_SUB):
            issue(g + EXPERT_AHEAD, ahead, tt, half, PEER_SLOTS)
            run_side(per_issue)
            cc = coef[:, tt:tt + 1]
            for c in range(chunks):
                oc = jnp.sum(tile(tt, chunks + c) * cc, axis=0, keepdims=True)
                o[c] = jnp.where(sub == tt, oc, o[c])
        run_side(len(pending))
        for c in range(chunks):
            out_ref[tok_rows, cols(c)] = x1_ref[tok_rows, cols(c)] + o[c]

    routed_gate_slot = (i + ROUTE_AHEAD) % gate_slots

    def ring_turn(q, carry):
        for r in range(EXPERT_RING):
            side = route_steps(h2r_ref, q * (EXPERT_RING // 2) + r // 2, r % 2, routed_gate_slot)
            group_body(q * EXPERT_RING + r, r, side)
        return carry

    lax.fori_loop(0, groups // EXPERT_RING, ring_turn, 0)
    publish_ids((i + ROUTE_AHEAD) % 2)

    @pl.when(i == last)
    def _drain():
        for a in range(EXPERT_AHEAD):
            wait((groups + a) % EXPERT_RING)


def _peer(h2, x1, peer_wq, peer_subkeys, peer_u, peer_v):
    T, D = h2.shape
    blocks = T // EXPERT_BLOCK
    groups = EXPERT_BLOCK // EXPERT_SUB
    sub_rows = EXPERT_SUB * PEER_SLOTS
    row_tiles = 2 * D // LANES
    n_experts = peer_u.shape[0]
    assert groups % EXPERT_RING == 0 and groups == 2 * PEER_HEADS and EXPERT_RING % 2 == 0
    assert groups & (groups - 1) == 0
    assert ROUTE_AHEAD == 2 and blocks >= ROUTE_AHEAD and n_experts % SUBLANES == 0
    assert EXPERT_BLOCK == LANES
    wq = peer_wq.astype(BF16).reshape(D, PEER_HEADS, 2 * PEER_HALF).transpose(1, 0, 2)
    sk = peer_subkeys.astype(BF16).reshape(2 * PEER_HEADS, PEER_KEYS, PEER_HALF)
    pos, pen = _candidate_tables()
    n_cand = pos.shape[0]
    table = _pack_table(peer_u, peer_v)
    table_tiles = table.reshape(n_experts // SUBLANES, row_tiles, SUBLANES, LANES)
    tok = lambda c: pl.BlockSpec((EXPERT_BLOCK, c), lambda i: (i, 0))
    const = lambda *shape: pl.BlockSpec(shape, lambda i: (0,) * len(shape))
    hbm = pl.BlockSpec(memory_space=pl.ANY)
    return pl.pallas_call(
        _peer_kernel,
        grid=(blocks,),
        in_specs=[tok(D),
                  pl.BlockSpec((EXPERT_BLOCK, D),
                               lambda i: (jnp.minimum(i + ROUTE_AHEAD, blocks - 1), 0)),
                  tok(D), hbm,
                  const(PEER_HEADS, D, 2 * PEER_HALF), const(2 * PEER_HEADS, PEER_KEYS, PEER_HALF),
                  const(n_cand, 1), const(n_cand, 1),
                  hbm, hbm],
        out_specs=tok(D),
        out_shape=jax.ShapeDtypeStruct((T, D), F32),
        scratch_shapes=[pltpu.VMEM((EXPERT_RING, sub_rows // SUBLANES, row_tiles, SUBLANES, LANES),
                                   F32),
                        pltpu.SemaphoreType.DMA((EXPERT_RING,)),
                        pltpu.SMEM((PEER_SLOTS, 2 * EXPERT_BLOCK), jnp.int32),
                        pltpu.VMEM((PEER_SLOTS, EXPERT_BLOCK), jnp.int32),
                        pltpu.VMEM((ROUTE_AHEAD + 1, PEER_SLOTS, EXPERT_BLOCK), F32),
                        pltpu.VMEM((EXPERT_BLOCK, D), F32),
                        pltpu.VMEM((PEER_TOPK, EXPERT_BLOCK), F32),
                        pltpu.VMEM((PEER_TOPK, EXPERT_BLOCK), F32),
                        pltpu.SemaphoreType.DMA((2,))],
        compiler_params=pltpu.CompilerParams(dimension_semantics=("arbitrary",),
                                             vmem_limit_bytes=VMEM_LIMIT),
        name="peer",
    )(h2, h2, x1, h2, wq, sk, jnp.asarray(pos), jnp.asarray(pen), table, table_tiles)


def kernel(x, norm1_g, w_in, q_norm_g, k_norm_g, attn_sinks, rel_bias, pool_w, pool_scale, w_out,
           norm2_g, peer_wq, peer_subkeys, peer_u, peer_v):
    batch, seq, d_model = x.shape
    depth = norm1_g.shape[0]
    assert seq % MIXER_ROWS == 0 and (batch * seq) % IN_PROJ_ROWS == 0
    assert (batch * seq) % EXPERT_BLOCK == 0 and peer_u.shape[1] % PACK_ROWS == 0
    x2 = x.reshape(batch * seq, d_model)
    for l in range(depth):
        q, kv, p = _in_proj(x2, norm1_g[l], w_in[l], q_norm_g[l], k_norm_g[l])
        x1, h2 = _mixer(x2, q, kv, p, rel_bias, attn_sinks[l], pool_w[l], pool_scale[l],
                        w_out[l], norm2_g[l], batch, seq)
        x2 = _peer(h2, x1, peer_wq[l], peer_subkeys[l], peer_u[l], peer_v[l])
    return x2.reshape(batch, seq, d_model)
```

```python
import math

import jax
import jax.numpy as jnp
import numpy as np
from jax import lax
from jax.experimental import pallas as pl
from jax.experimental.pallas import tpu as pltpu

F32 = jnp.float32
BF16 = jnp.bfloat16

CHUNK = 64
ATTN_HEADS = 8
ATTN_KV_HEADS = 2
HEAD_DIM = 64
ATTN_GROUP = ATTN_HEADS // ATTN_KV_HEADS
WIN_CHUNKS = 2
BAND = (WIN_CHUNKS + 1) * CHUNK
D_ATTN = ATTN_HEADS * HEAD_DIM
D_KV = ATTN_KV_HEADS * HEAD_DIM
POOL_WINDOWS = (2, 4, 8, 16)
POOL_GROUP_DIM = 128
D_POOL = len(POOL_WINDOWS) * POOL_GROUP_DIM
REL_BUCKETS = 32
REL_MAX_DIST = 128
PEER_KEYS = 128
PEER_HEADS = 8
PEER_TOPK = 16
PEER_HALF = 128
PEER_SLOTS = PEER_HEADS * PEER_TOPK
EPS = 1e-6
NEG_INF = -1e30

LANES = 128
SUBLANES = 8

IN_PROJ_ROWS = 512
MIXER_ROWS = 256
POOL_HALO = 16
KV_HALO = WIN_CHUNKS * CHUNK
PACK_ROWS = 256
EXPERT_BLOCK = 128
EXPERT_SUB = 8
EXPERT_AHEAD = 3
EXPERT_RING = EXPERT_AHEAD + 1
ROUTE_AHEAD = 2
VMEM_LIMIT = 56 * 1024 * 1024


def _rms(x, g):
    return x * lax.rsqrt(jnp.mean(x * x, axis=-1, keepdims=True) + EPS) * g


def _segment_sumsq(x, ind):
    sq = x * x
    hi = sq.astype(BF16)
    lo = (sq - hi.astype(F32)).astype(BF16)
    return (jnp.dot(hi, ind, preferred_element_type=F32)
            + jnp.dot(lo, ind, preferred_element_type=F32))


def _in_proj_kernel(x_ref, g1_ref, w_ref, gq_ref, gk_ref, indq_ref, indk_ref,
                    q_ref, kv_ref, p_ref):
    h = _rms(x_ref[...], g1_ref[...])
    y = jnp.dot(h.astype(BF16), w_ref[...], preferred_element_type=F32)
    q = y[:, :D_ATTN]
    k = y[:, D_ATTN:D_ATTN + D_KV]
    v = y[:, D_ATTN + D_KV:D_ATTN + 2 * D_KV]
    qn = q * lax.rsqrt(_segment_sumsq(q, indq_ref[...]) * (1.0 / HEAD_DIM) + EPS) * gq_ref[...]
    kn = k * lax.rsqrt(_segment_sumsq(k, indk_ref[...]) * (1.0 / HEAD_DIM) + EPS) * gk_ref[...]
    q_ref[...] = (qn * (HEAD_DIM ** -0.5)).astype(BF16)
    kv_ref[:, :D_KV] = kn.astype(BF16)
    kv_ref[:, D_KV:] = v.astype(BF16)
    p_ref[...] = y[:, D_ATTN + 2 * D_KV:]


def _in_proj(x2, g1, w_in, gq, gk):
    T, D = x2.shape
    d_in = w_in.shape[1]
    head_of = np.arange(D_ATTN) // HEAD_DIM
    indq = jnp.asarray(head_of[:, None] == head_of[None, :], BF16)
    indk = indq[:D_KV, :D_KV]
    full = lambda r, c: pl.BlockSpec((r, c), lambda i: (0, 0))
    rows = lambda c: pl.BlockSpec((IN_PROJ_ROWS, c), lambda i: (i, 0))
    return pl.pallas_call(
        _in_proj_kernel,
        grid=(T // IN_PROJ_ROWS,),
        in_specs=[rows(D), full(1, D), full(D, d_in), full(1, D_ATTN), full(1, D_KV),
                  full(D_ATTN, D_ATTN), full(D_KV, D_KV)],
        out_specs=[rows(D_ATTN), rows(2 * D_KV), rows(D_POOL)],
        out_shape=[jax.ShapeDtypeStruct((T, D_ATTN), BF16),
                   jax.ShapeDtypeStruct((T, 2 * D_KV), BF16),
                   jax.ShapeDtypeStruct((T, D_POOL), F32)],
        compiler_params=pltpu.CompilerParams(dimension_semantics=("arbitrary",),
                                             vmem_limit_bytes=VMEM_LIMIT),
        name="in_proj",
    )(x2, g1.reshape(1, D), w_in.astype(BF16),
      jnp.tile(gq, ATTN_HEADS).reshape(1, D_ATTN), jnp.tile(gk, ATTN_KV_HEADS).reshape(1, D_KV),
      indq, indk)


def _t5_bucket_table():
    i = np.arange(CHUNK)[:, None]
    j = np.arange(BAND)[None, :]
    rel = (j - WIN_CHUNKS * CHUNK) - i
    nb = REL_BUCKETS // 2
    max_exact = nb // 2
    base = np.where(rel > 0, nb, 0)
    n = np.abs(rel)
    nf = np.maximum(n, 1).astype(np.float64)
    large = max_exact + (np.log(nf / max_exact) / math.log(REL_MAX_DIST / max_exact)
                         * (nb - max_exact)).astype(np.int32)
    large = np.minimum(large, nb - 1)
    return (base + np.where(n < max_exact, n, large)).astype(np.int32)


def _mixer_kernel(x_ref, q_ref, kvc_ref, kvp_ref, pc_ref, pp_ref, bkt_ref, relb_ref, sink_ref,
                  poolw_ref, pscale_ref, wout_ref, g2_ref,
                  x1_ref, h2_ref,
                  bias_scr, kv_scr, p_scr):
    b = pl.program_id(0)
    i = pl.program_id(1)
    rows = x_ref.shape[0]

    @pl.when((b == 0) & (i == 0))
    def _build_bias():
        bkt = bkt_ref[...]
        for h in range(ATTN_HEADS):
            acc = jnp.zeros((CHUNK, BAND), F32)
            for bucket in range(REL_BUCKETS):
                acc = jnp.where(bkt == bucket, relb_ref[bucket, h], acc)
            bias_scr[h] = acc

    kv_scr[:KV_HALO] = kvp_ref[...]
    kv_scr[KV_HALO:] = kvc_ref[...]
    p_scr[:POOL_HALO] = jnp.where(i == 0, 0.0, pp_ref[...])
    p_scr[POOL_HALO:] = pc_ref[...]

    key_slot = lax.broadcasted_iota(jnp.int32, (CHUNK, BAND), 1) // CHUNK
    attn_chunks = []
    for c in range(rows // CHUNK):
        first_key_chunk = i * (rows // CHUNK) + c - WIN_CHUNKS
        valid = (key_slot + first_key_chunk) >= 0
        band = slice(c * CHUNK, c * CHUNK + BAND)
        scores = []
        for h in range(ATTN_HEADS):
            kvh = h // ATTN_GROUP
            qh = q_ref[c * CHUNK:(c + 1) * CHUNK, h * HEAD_DIM:(h + 1) * HEAD_DIM]
            kb = kv_scr[band, kvh * HEAD_DIM:(kvh + 1) * HEAD_DIM]
            s = lax.dot_general(qh, kb, (((1,), (1,)), ((), ())), preferred_element_type=F32)
            scores.append(jnp.where(valid, s + bias_scr[h], NEG_INF))
        probs = []
        for h in range(ATTN_HEADS):
            s = scores[h]
            sink = sink_ref[h]
            m = jnp.maximum(jnp.max(s, axis=-1, keepdims=True), sink)
            e = jnp.exp(s - m)
            denom = jnp.sum(e, axis=-1, keepdims=True) + jnp.exp(sink - m)
            probs.append((e / denom).astype(BF16))
        heads_out = []
        for h in range(ATTN_HEADS):
            kvh = h // ATTN_GROUP
            vb = kv_scr[band, D_KV + kvh * HEAD_DIM:D_KV + (kvh + 1) * HEAD_DIM]
            heads_out.append(jnp.dot(probs[h], vb, preferred_element_type=F32))
        attn_chunks.append(jnp.concatenate(heads_out, axis=1).astype(BF16))
    attn = jnp.concatenate(attn_chunks, axis=0)

    t_seq = i * rows + lax.broadcasted_iota(jnp.int32, (rows, 1), 0)
    pooled_out = []
    for g, w in enumerate(POOL_WINDOWS):
        lanes = slice(g * POOL_GROUP_DIM, (g + 1) * POOL_GROUP_DIM)
        cur = p_scr[POOL_HALO:POOL_HALO + rows, lanes]
        total = cur
        for j in range(1, w):
            total = total + p_scr[POOL_HALO - j:POOL_HALO - j + rows, lanes]
        cnt = jnp.minimum(t_seq + 1, w).astype(F32)
        d = total / cnt - cur
        og = jnp.dot(d.astype(BF16), poolw_ref[g], preferred_element_type=F32)
        pooled_out.append(og * pscale_ref[:, lanes])
    pooled = jnp.concatenate(pooled_out, axis=1)

    mixed = (jnp.dot(attn, wout_ref[:D_ATTN], preferred_element_type=F32)
             + jnp.dot(pooled.astype(BF16), wout_ref[D_ATTN:], preferred_element_type=F32))
    x1 = x_ref[...] + mixed
    x1_ref[...] = x1
    h2_ref[...] = _rms(x1, g2_ref[...])


def _mixer(x2, q, kv, p, rel_bias, sinks, pool_w, pool_scale, w_out, g2, batch, seq):
    T, D = x2.shape
    nq = seq // MIXER_ROWS
    cur = lambda c: pl.BlockSpec((MIXER_ROWS, c), lambda b, i: (b * nq + i, 0))
    full2 = lambda r, c: pl.BlockSpec((r, c), lambda b, i: (0, 0))

    def prev(rows_per_block, c):
        per = MIXER_ROWS // rows_per_block
        return pl.BlockSpec((rows_per_block, c),
                            lambda b, i: (jnp.maximum((b * nq + i) * per - 1, 0), 0))

    smem = pl.BlockSpec(memory_space=pltpu.SMEM)
    return pl.pallas_call(
        _mixer_kernel,
        grid=(batch, nq),
        in_specs=[cur(D), cur(D_ATTN), cur(2 * D_KV), prev(KV_HALO, 2 * D_KV),
                  cur(D_POOL), prev(POOL_HALO, D_POOL),
                  full2(CHUNK, BAND), smem, smem,
                  pl.BlockSpec((len(POOL_WINDOWS), POOL_GROUP_DIM, POOL_GROUP_DIM),
                               lambda b, i: (0, 0, 0)),
                  full2(1, D_POOL), full2(D, D), full2(1, D)],
        out_specs=[cur(D), cur(D)],
        out_shape=[jax.ShapeDtypeStruct((T, D), F32), jax.ShapeDtypeStruct((T, D), F32)],
        scratch_shapes=[pltpu.VMEM((ATTN_HEADS, CHUNK, BAND), F32),
                        pltpu.VMEM((KV_HALO + MIXER_ROWS, 2 * D_KV), BF16),
                        pltpu.VMEM((POOL_HALO + MIXER_ROWS, D_POOL), F32)],
        compiler_params=pltpu.CompilerParams(dimension_semantics=("arbitrary", "arbitrary"),
                                             vmem_limit_bytes=VMEM_LIMIT),
        name="mixer",
    )(x2, q, kv, kv, p, p, jnp.asarray(_t5_bucket_table()), rel_bias.astype(F32),
      sinks.astype(F32), pool_w.astype(BF16), pool_scale.reshape(1, D_POOL),
      w_out.astype(BF16), g2.reshape(1, D))


def _pack_kernel(u_ref, v_ref, out_ref):
    rows = u_ref.shape[0]
    tiles = u_ref.shape[1] // LANES
    for t, src in enumerate((u_ref, v_ref)):
        for c in range(tiles):
            out_ref[pl.ds(t * tiles + c, rows, stride=2 * tiles), :] = src[:, c * LANES:(c + 1) * LANES]


def _pack_table(peer_u, peer_v):
    n, d = peer_u.shape
    row_tiles = 2 * d // LANES
    packed = pl.pallas_call(
        _pack_kernel,
        grid=(n // PACK_ROWS,),
        in_specs=[pl.BlockSpec((PACK_ROWS, d), lambda i: (i, 0)),
                  pl.BlockSpec((PACK_ROWS, d), lambda i: (i, 0))],
        out_specs=pl.BlockSpec((PACK_ROWS * row_tiles, LANES), lambda i: (i, 0)),
        out_shape=jax.ShapeDtypeStruct((n * row_tiles, LANES), F32),
        compiler_params=pltpu.CompilerParams(dimension_semantics=("arbitrary",),
                                             vmem_limit_bytes=VMEM_LIMIT),
        name="pack_table",
    )(peer_u, peer_v)
    return packed.reshape(n, row_tiles, LANES)


def _gelu(x):
    return 0.5 * x * (1.0 + lax.erf(x * (2.0 ** -0.5)))


def _top_round_steps(state, scores, ident, values, winners):
    row = lax.broadcasted_iota(jnp.int32, (PEER_TOPK, LANES), 0)

    def make(r):
        def step():
            s, ids = state[scores], state[ident]
            m = jnp.max(s, axis=0, keepdims=True)
            idx = jnp.min(jnp.where(s == m, ids, 1e9), axis=0, keepdims=True)
            state[values] = jnp.where(row == r, m, state[values])
            state[winners] = jnp.where(row == r, idx, state[winners])
            state[scores] = jnp.where(ids == idx, -jnp.inf, s)
        return step

    return [make(r) for r in range(PEER_TOPK)]


def _candidate_tables():
    pos, pen = [], []
    for a, width in [(0, PEER_TOPK)] + [(a, SUBLANES) for a in range(1, SUBLANES)]:
        for b in range(width):
            pos.append(a * PEER_TOPK + b)
            pen.append(0.0 if (a + 1) * (b + 1) <= PEER_TOPK else -np.inf)
    for a in range(SUBLANES, PEER_TOPK):
        pos.append(a * PEER_TOPK)
        pen.append(0.0)
    return (np.asarray(pos, np.float32).reshape(-1, 1), np.asarray(pen, np.float32).reshape(-1, 1))


def _route_steps(hr_ref, wq_ref, sk_ref, pos_ref, pen_ref, h, p, val_scr, idx_scr, ids_ref, gates_ref):
    st = {}
    zeros = jnp.zeros((PEER_TOPK, LANES), F32)

    def start():
        q = jnp.dot(hr_ref[...].astype(BF16), wq_ref[h, :, p * PEER_HALF:(p + 1) * PEER_HALF],
                    preferred_element_type=F32).astype(BF16)
        st["scores"] = lax.dot_general(sk_ref[2 * h + p], q, (((1,), (1,)), ((), ())),
                                       preferred_element_type=F32)
        st["keys"] = lax.broadcasted_iota(jnp.int32, (PEER_KEYS, LANES), 0).astype(F32)
        st["val"], st["idx"] = zeros, zeros

    steps = [start] + _top_round_steps(st, "scores", "keys", "val", "idx")

    if p == 0:
        def finish():
            val_scr[...] = st["val"]
            idx_scr[...] = st["idx"]

        return steps + [finish]

    def start_head():
        v1, v2 = val_scr[...], st["val"]
        pen = jnp.broadcast_to(pen_ref[...], (pen_ref.shape[0], LANES))
        st["cand"] = jnp.concatenate(
            [v1[0:1, :] + v2]
            + [v1[a:a + 1, :] + v2[0:SUBLANES, :] for a in range(1, SUBLANES)]
            + [v1[SUBLANES:, :] + v2[0:1, :]], axis=0) + pen
        st["pos"] = jnp.broadcast_to(pos_ref[...], (pos_ref.shape[0], LANES))
        st["best"], st["win"] = zeros, zeros

    def finish_head():
        i1, i2 = idx_scr[...], st["idx"]
        pos = st["win"].astype(jnp.int32)
        pa = pos // PEER_TOPK
        pb = pos % PEER_TOPK
        e1, e2 = zeros, zeros
        for a in range(PEER_TOPK):
            e1 = e1 + jnp.where(pa == a, i1[a:a + 1, :], 0.0)
            e2 = e2 + jnp.where(pb == a, i2[a:a + 1, :], 0.0)
        best = st["best"]
        ex = jnp.exp(best - best[0:1, :])
        head_rows = pl.ds(pl.multiple_of(h * PEER_TOPK, PEER_TOPK), PEER_TOPK)
        ids_ref[head_rows, :] = (e1 * PEER_KEYS + e2).astype(jnp.int32)
        gates_ref[head_rows, :] = ex / jnp.sum(ex, axis=0, keepdims=True)

    return (steps + [start_head] + _top_round_steps(st, "cand", "pos", "best", "win")
            + [finish_head])


def _peer_kernel(h2_ref, h2r_ref, x1_ref, h2_hbm, wq_ref, sk_ref, pos_ref, pen_ref,
                 tab_hbm, tab_tiles_hbm, out_ref,
                 ring, ring_sems, ids_smem, ids_stage, gates_ring, hr_scr,
                 val_scr, idx_scr, aux_sems):
    i = pl.program_id(0)
    last = pl.num_programs(0) - 1
    d = h2_ref.shape[1]
    chunks = d // LANES
    groups = EXPERT_BLOCK // EXPERT_SUB
    tok_tiles = PEER_SLOTS // SUBLANES
    half = PEER_SLOTS // 2
    gate_slots = gates_ring.shape[0]

    def route_steps(hr_ref, h, p, stage, gate_slot):
        return _route_steps(hr_ref, wq_ref, sk_ref, pos_ref, pen_ref, h, p, val_scr, idx_scr,
                            ids_stage.at[stage], gates_ring.at[gate_slot])

    def ids_copy(stage, parity):
        col0 = parity * EXPERT_BLOCK
        if not isinstance(col0, int):
            col0 = pl.multiple_of(col0, EXPERT_BLOCK)
        return pltpu.make_async_copy(ids_stage.at[stage],
                                     ids_smem.at[:, pl.ds(col0, EXPERT_BLOCK)], aux_sems.at[0])

    def issue(group, slot, tt, k0, k1):
        if isinstance(group, int):
            block_off, in_block = divmod(group, groups)
        else:
            block_off, in_block = group >> (groups.bit_length() - 1), group & (groups - 1)
        col = ((i + block_off) & 1) * EXPERT_BLOCK + in_block * EXPERT_SUB + tt
        for k in range(k0, k1):
            row = ids_smem[k, col]
            dst = ring.at[slot, tt * tok_tiles + k // SUBLANES, :, k % SUBLANES]
            pltpu.make_async_copy(tab_hbm.at[row], dst, ring_sems.at[slot]).start(priority=k % 2)

    def wait(slot):
        pltpu.make_async_copy(tab_tiles_hbm.at[pl.ds(0, ring.shape[1])], ring.at[slot],
                              ring_sems.at[slot]).wait()

    @pl.when(i == 0)
    def _prologue():
        for blk in range(ROUTE_AHEAD):
            if blk == 0:
                src = h2_ref
            else:
                copy = pltpu.make_async_copy(h2_hbm.at[pl.ds(blk * EXPERT_BLOCK, EXPERT_BLOCK)],
                                             hr_scr, aux_sems.at[1])
                copy.start()
                copy.wait()
                src = hr_scr

            def head(h, carry, src=src, blk=blk):
                for p in range(2):
                    for step in route_steps(src, h, p, 0, blk):
                        step()
                return carry

            lax.fori_loop(0, PEER_HEADS, head, 0)
            copy = ids_copy(0, blk)
            copy.start()
            copy.wait()
        for a in range(EXPERT_AHEAD):
            for tt in range(EXPERT_SUB):
                issue(a, a, tt, 0, PEER_SLOTS)

    lane = lax.broadcasted_iota(jnp.int32, (PEER_SLOTS, LANES), 1)
    sub = lax.broadcasted_iota(jnp.int32, (EXPERT_SUB, LANES), 0)

    def group_body(g, slot, side_steps):
        ahead = (slot + EXPERT_AHEAD) % EXPERT_RING
        per_issue = -(-len(side_steps) // (2 * EXPERT_SUB))
        pending = list(side_steps)

        def run_side(n):
            for _ in range(min(n, len(pending))):
                pending.pop(0)()

        wait(slot)
        tok_rows = pl.ds(pl.multiple_of(g * EXPERT_SUB, EXPERT_SUB), EXPERT_SUB)
        hsub = h2_ref[tok_rows, :]

        def tile(tt, c):
            t0 = tt * tok_tiles
            return ring[slot, t0:t0 + tok_tiles, c].reshape(PEER_SLOTS, LANES)

        def cols(j):
            return slice(j * LANES, (j + 1) * LANES)

        s = jnp.zeros((PEER_SLOTS, LANES), F32)
        for tt in range(EXPERT_SUB):
            issue(g + EXPERT_AHEAD, ahead, tt, 0, half)
            run_side(per_issue)
            acc = tile(tt, 0) * hsub[tt:tt + 1, cols(0)]
            for c in range(1, chunks):
                acc = acc + tile(tt, c) * hsub[tt:tt + 1, cols(c)]
            s = jnp.where(lane == tt, jnp.sum(acc, axis=1, keepdims=True), s)
        gates = pltpu.roll(gates_ring[i % gate_slots],
                           (EXPERT_BLOCK - g * EXPERT_SUB) % EXPERT_BLOCK, 1)
        coef = gates * _gelu(s)
        o = [jnp.zeros((EXPERT_SUB, LANES), F32) for _ in range(chunks)]
        for tt in range(EXPERT_SUB):
            issue(g + EXPERT_AHEAD, ahead, tt, half, PEER_SLOTS)
            run_side(per_issue)
            cc = coef[:, tt:tt + 1]
            for c in range(chunks):
                oc = jnp.sum(tile(tt, chunks + c) * cc, axis=0, keepdims=True)
                o[c] = jnp.where(sub == tt, oc, o[c])
        run_side(len(pending))
        for c in range(chunks):
            out_ref[tok_rows, cols(c)] = x1_ref[tok_rows, cols(c)] + o[c]

    routed_gate_slot = (i + ROUTE_AHEAD) % gate_slots

    turns = groups // EXPERT_RING

    def ring_turn(q, carry):
        @pl.when((q == turns - 1) & (i > 0))
        def _ids_landed():
            ids_copy(0, 0).wait()

        for r in range(EXPERT_RING):
            side = route_steps(h2r_ref, q * (EXPERT_RING // 2) + r // 2, r % 2, i % 2,
                               routed_gate_slot)
            group_body(q * EXPERT_RING + r, r, side)
        return carry

    lax.fori_loop(0, turns, ring_turn, 0)
    ids_copy(i % 2, i % 2).start()

    @pl.when(i == last)
    def _drain():
        ids_copy(0, 0).wait()
        for a in range(EXPERT_AHEAD):
            wait((groups + a) % EXPERT_RING)


def _peer(h2, x1, peer_wq, peer_subkeys, peer_u, peer_v):
    T, D = h2.shape
    blocks = T // EXPERT_BLOCK
    groups = EXPERT_BLOCK // EXPERT_SUB
    sub_rows = EXPERT_SUB * PEER_SLOTS
    row_tiles = 2 * D // LANES
    n_experts = peer_u.shape[0]
    assert groups % EXPERT_RING == 0 and groups == 2 * PEER_HEADS and EXPERT_RING % 2 == 0
    assert groups & (groups - 1) == 0
    assert ROUTE_AHEAD == 2 and blocks >= ROUTE_AHEAD and n_experts % SUBLANES == 0
    assert EXPERT_BLOCK == LANES
    wq = peer_wq.astype(BF16).reshape(D, PEER_HEADS, 2 * PEER_HALF).transpose(1, 0, 2)
    sk = peer_subkeys.astype(BF16).reshape(2 * PEER_HEADS, PEER_KEYS, PEER_HALF)
    pos, pen = _candidate_tables()
    n_cand = pos.shape[0]
    table = _pack_table(peer_u, peer_v)
    table_tiles = table.reshape(n_experts // SUBLANES, row_tiles, SUBLANES, LANES)
    tok = lambda c: pl.BlockSpec((EXPERT_BLOCK, c), lambda i: (i, 0))
    const = lambda *shape: pl.BlockSpec(shape, lambda i: (0,) * len(shape))
    hbm = pl.BlockSpec(memory_space=pl.ANY)
    return pl.pallas_call(
        _peer_kernel,
        grid=(blocks,),
        in_specs=[tok(D),
                  pl.BlockSpec((EXPERT_BLOCK, D),
                               lambda i: (jnp.minimum(i + ROUTE_AHEAD, blocks - 1), 0)),
                  tok(D), hbm,
                  const(PEER_HEADS, D, 2 * PEER_HALF), const(2 * PEER_HEADS, PEER_KEYS, PEER_HALF),
                  const(n_cand, 1), const(n_cand, 1),
                  hbm, hbm],
        out_specs=tok(D),
        out_shape=jax.ShapeDtypeStruct((T, D), F32),
        scratch_shapes=[pltpu.VMEM((EXPERT_RING, sub_rows // SUBLANES, row_tiles, SUBLANES, LANES),
                                   F32),
                        pltpu.SemaphoreType.DMA((EXPERT_RING,)),
                        pltpu.SMEM((PEER_SLOTS, 2 * EXPERT_BLOCK), jnp.int32),
                        pltpu.VMEM((2, PEER_SLOTS, EXPERT_BLOCK), jnp.int32),
                        pltpu.VMEM((ROUTE_AHEAD + 1, PEER_SLOTS, EXPERT_BLOCK), F32),
                        pltpu.VMEM((EXPERT_BLOCK, D), F32),
                        pltpu.VMEM((PEER_TOPK, EXPERT_BLOCK), F32),
                        pltpu.VMEM((PEER_TOPK, EXPERT_BLOCK), F32),
                        pltpu.SemaphoreType.DMA((2,))],
        compiler_params=pltpu.CompilerParams(dimension_semantics=("arbitrary",),
                                             vmem_limit_bytes=VMEM_LIMIT),
        name="peer",
    )(h2, h2, x1, h2, wq, sk, jnp.asarray(pos), jnp.asarray(pen), table, table_tiles)


def kernel(x, norm1_g, w_in, q_norm_g, k_norm_g, attn_sinks, rel_bias, pool_w, pool_scale, w_out,
           norm2_g, peer_wq, peer_subkeys, peer_u, peer_v):
    batch, seq, d_model = x.shape
    depth = norm1_g.shape[0]
    assert seq % MIXER_ROWS == 0 and (batch * seq) % IN_PROJ_ROWS == 0
    assert (batch * seq) % EXPERT_BLOCK == 0 and peer_u.shape[1] % PACK_ROWS == 0
    x2 = x.reshape(batch * seq, d_model)
    for l in range(depth):
        q, kv, p = _in_proj(x2, norm1_g[l], w_in[l], q_norm_g[l], k_norm_g[l])
        x1, h2 = _mixer(x2, q, kv, p, rel_bias, attn_sinks[l], pool_w[l], pool_scale[l],
                        w_out[l], norm2_g[l], batch, seq)
        x2 = _peer(h2, x1, peer_wq[l], peer_subkeys[l], peer_u[l], peer_v[l])
    return x2.reshape(batch, seq, d_model)
```

```python
import functools
import math

import jax
import jax.numpy as jnp
import numpy as np
from jax import lax
from jax.experimental import pallas as pl
from jax.experimental.pallas import tpu as pltpu

F32 = jnp.float32
BF16 = jnp.bfloat16

CHUNK = 64
ATTN_HEADS = 8
ATTN_KV_HEADS = 2
HEAD_DIM = 64
ATTN_GROUP = ATTN_HEADS // ATTN_KV_HEADS
WIN_CHUNKS = 2
BAND = (WIN_CHUNKS + 1) * CHUNK
D_ATTN = ATTN_HEADS * HEAD_DIM
D_KV = ATTN_KV_HEADS * HEAD_DIM
POOL_WINDOWS = (2, 4, 8, 16)
POOL_GROUP_DIM = 128
D_POOL = len(POOL_WINDOWS) * POOL_GROUP_DIM
REL_BUCKETS = 32
REL_MAX_DIST = 128
PEER_KEYS = 128
PEER_HEADS = 8
PEER_TOPK = 16
PEER_HALF = 128
PEER_SLOTS = PEER_HEADS * PEER_TOPK
EPS = 1e-6
NEG_INF = -1e30

LANES = 128
SUBLANES = 8

IN_PROJ_ROWS = 512
POOL_HALO = 16
KV_HALO = WIN_CHUNKS * CHUNK
PACK_ROWS = 256
BLOCK = 128
EXPERT_SUB = 8
EXPERT_AHEAD = 3
EXPERT_RING = EXPERT_AHEAD + 1
ROUTE_AHEAD = 2
MIX_AHEAD = ROUTE_AHEAD + 1
MIX_RING = MIX_AHEAD + 1
VMEM_LIMIT = 58 * 1024 * 1024


def _rms(x, g):
    return x * lax.rsqrt(jnp.mean(x * x, axis=-1, keepdims=True) + EPS) * g


def _segment_sumsq(x, ind):
    sq = x * x
    hi = sq.astype(BF16)
    lo = (sq - hi.astype(F32)).astype(BF16)
    return (jnp.dot(hi, ind, preferred_element_type=F32)
            + jnp.dot(lo, ind, preferred_element_type=F32))


def _in_proj_kernel(x_ref, g1_ref, w_ref, gq_ref, gk_ref, indq_ref, indk_ref,
                    q_ref, kv_ref, p_ref):
    h = _rms(x_ref[...], g1_ref[...])
    y = jnp.dot(h.astype(BF16), w_ref[...], preferred_element_type=F32)
    q = y[:, :D_ATTN]
    k = y[:, D_ATTN:D_ATTN + D_KV]
    v = y[:, D_ATTN + D_KV:D_ATTN + 2 * D_KV]
    qn = q * lax.rsqrt(_segment_sumsq(q, indq_ref[...]) * (1.0 / HEAD_DIM) + EPS) * gq_ref[...]
    kn = k * lax.rsqrt(_segment_sumsq(k, indk_ref[...]) * (1.0 / HEAD_DIM) + EPS) * gk_ref[...]
    q_ref[...] = (qn * (HEAD_DIM ** -0.5)).astype(BF16)
    kv_ref[:, :D_KV] = kn.astype(BF16)
    kv_ref[:, D_KV:] = v.astype(BF16)
    p_ref[...] = y[:, D_ATTN + 2 * D_KV:]


def _in_proj(x2, g1, w_in, gq, gk):
    T, D = x2.shape
    d_in = w_in.shape[1]
    head_of = np.arange(D_ATTN) // HEAD_DIM
    indq = jnp.asarray(head_of[:, None] == head_of[None, :], BF16)
    indk = indq[:D_KV, :D_KV]
    full = lambda r, c: pl.BlockSpec((r, c), lambda i: (0, 0))
    rows = lambda c: pl.BlockSpec((IN_PROJ_ROWS, c), lambda i: (i, 0))
    return pl.pallas_call(
        _in_proj_kernel,
        grid=(T // IN_PROJ_ROWS,),
        in_specs=[rows(D), full(1, D), full(D, d_in), full(1, D_ATTN), full(1, D_KV),
                  full(D_ATTN, D_ATTN), full(D_KV, D_KV)],
        out_specs=[rows(D_ATTN), rows(2 * D_KV), rows(D_POOL)],
        out_shape=[jax.ShapeDtypeStruct((T, D_ATTN), BF16),
                   jax.ShapeDtypeStruct((T, 2 * D_KV), BF16),
                   jax.ShapeDtypeStruct((T, D_POOL), F32)],
        compiler_params=pltpu.CompilerParams(dimension_semantics=("arbitrary",),
                                             vmem_limit_bytes=VMEM_LIMIT),
        name="in_proj",
    )(x2, g1.reshape(1, D), w_in.astype(BF16),
      jnp.tile(gq, ATTN_HEADS).reshape(1, D_ATTN), jnp.tile(gk, ATTN_KV_HEADS).reshape(1, D_KV),
      indq, indk)


def _pack_kernel(u_ref, v_ref, out_ref):
    rows = u_ref.shape[0]
    tiles = u_ref.shape[1] // LANES
    for t, src in enumerate((u_ref, v_ref)):
        for c in range(tiles):
            out_ref[pl.ds(t * tiles + c, rows, stride=2 * tiles), :] = src[:, c * LANES:(c + 1) * LANES]


def _pack_table(peer_u, peer_v):
    n, d = peer_u.shape
    row_tiles = 2 * d // LANES
    packed = pl.pallas_call(
        _pack_kernel,
        grid=(n // PACK_ROWS,),
        in_specs=[pl.BlockSpec((PACK_ROWS, d), lambda i: (i, 0)),
                  pl.BlockSpec((PACK_ROWS, d), lambda i: (i, 0))],
        out_specs=pl.BlockSpec((PACK_ROWS * row_tiles, LANES), lambda i: (i, 0)),
        out_shape=jax.ShapeDtypeStruct((n * row_tiles, LANES), F32),
        compiler_params=pltpu.CompilerParams(dimension_semantics=("arbitrary",),
                                             vmem_limit_bytes=VMEM_LIMIT),
        name="pack_table",
    )(peer_u, peer_v)
    return packed.reshape(n, row_tiles, LANES)


def _t5_bucket_table():
    i = np.arange(CHUNK)[:, None]
    j = np.arange(BAND)[None, :]
    rel = (j - WIN_CHUNKS * CHUNK) - i
    nb = REL_BUCKETS // 2
    max_exact = nb // 2
    base = np.where(rel > 0, nb, 0)
    n = np.abs(rel)
    nf = np.maximum(n, 1).astype(np.float64)
    large = max_exact + (np.log(nf / max_exact) / math.log(REL_MAX_DIST / max_exact)
                         * (nb - max_exact)).astype(np.int32)
    large = np.minimum(large, nb - 1)
    return (base + np.where(n < max_exact, n, large)).astype(np.int32)


def _build_bias(bkt_ref, relb_ref, bias_scr):
    bkt = bkt_ref[...]
    for h in range(ATTN_HEADS):
        acc = jnp.zeros((CHUNK, BAND), F32)
        for bucket in range(REL_BUCKETS):
            acc = jnp.where(bkt == bucket, relb_ref[bucket, h], acc)
        bias_scr[h] = acc


def _mixer_block(x_ref, q_ref, kvc_ref, kvp_ref, pc_ref, pp_ref, sink_ref, poolw_ref, pscale_ref,
                 wout_ref, g2_ref, bias_scr, kv_scr, p_scr, block_in_seq, x1_out, h2_out):
    rows = x_ref.shape[0]
    kv_scr[:KV_HALO] = kvp_ref[...]
    kv_scr[KV_HALO:] = kvc_ref[...]
    p_scr[:POOL_HALO] = jnp.where(block_in_seq == 0, 0.0, pp_ref[...])
    p_scr[POOL_HALO:] = pc_ref[...]

    key_slot = lax.broadcasted_iota(jnp.int32, (CHUNK, BAND), 1) // CHUNK
    attn_chunks = []
    for c in range(rows // CHUNK):
        first_key_chunk = block_in_seq * (rows // CHUNK) + c - WIN_CHUNKS
        valid = (key_slot + first_key_chunk) >= 0
        band = slice(c * CHUNK, c * CHUNK + BAND)
        scores = []
        for h in range(ATTN_HEADS):
            kvh = h // ATTN_GROUP
            qh = q_ref[c * CHUNK:(c + 1) * CHUNK, h * HEAD_DIM:(h + 1) * HEAD_DIM]
            kb = kv_scr[band, kvh * HEAD_DIM:(kvh + 1) * HEAD_DIM]
            s = lax.dot_general(qh, kb, (((1,), (1,)), ((), ())), preferred_element_type=F32)
            scores.append(jnp.where(valid, s + bias_scr[h], NEG_INF))
        probs = []
        for h in range(ATTN_HEADS):
            s = scores[h]
            sink = sink_ref[h]
            m = jnp.maximum(jnp.max(s, axis=-1, keepdims=True), sink)
            e = jnp.exp(s - m)
            denom = jnp.sum(e, axis=-1, keepdims=True) + jnp.exp(sink - m)
            probs.append((e / denom).astype(BF16))
        heads_out = []
        for h in range(ATTN_HEADS):
            kvh = h // ATTN_GROUP
            vb = kv_scr[band, D_KV + kvh * HEAD_DIM:D_KV + (kvh + 1) * HEAD_DIM]
            heads_out.append(jnp.dot(probs[h], vb, preferred_element_type=F32))
        attn_chunks.append(jnp.concatenate(heads_out, axis=1).astype(BF16))
    attn = jnp.concatenate(attn_chunks, axis=0)

    t_seq = block_in_seq * rows + lax.broadcasted_iota(jnp.int32, (rows, 1), 0)
    pooled_out = []
    for g, w in enumerate(POOL_WINDOWS):
        lanes = slice(g * POOL_GROUP_DIM, (g + 1) * POOL_GROUP_DIM)
        cur = p_scr[POOL_HALO:POOL_HALO + rows, lanes]
        total = cur
        for j in range(1, w):
            total = total + p_scr[POOL_HALO - j:POOL_HALO - j + rows, lanes]
        cnt = jnp.minimum(t_seq + 1, w).astype(F32)
        d = total / cnt - cur
        og = jnp.dot(d.astype(BF16), poolw_ref[g], preferred_element_type=F32)
        pooled_out.append(og * pscale_ref[:, lanes])
    pooled = jnp.concatenate(pooled_out, axis=1)

    mixed = (jnp.dot(attn, wout_ref[:D_ATTN], preferred_element_type=F32)
             + jnp.dot(pooled.astype(BF16), wout_ref[D_ATTN:], preferred_element_type=F32))
    x1 = x_ref[...] + mixed
    x1_out[...] = x1
    h2_out[...] = _rms(x1, g2_ref[...])


def _gelu(x):
    return 0.5 * x * (1.0 + lax.erf(x * (2.0 ** -0.5)))


def _top_round_steps(state, scores, ident, values, winners):
    row = lax.broadcasted_iota(jnp.int32, (PEER_TOPK, LANES), 0)

    def make(r):
        def step():
            s, ids = state[scores], state[ident]
            m = jnp.max(s, axis=0, keepdims=True)
            idx = jnp.min(jnp.where(s == m, ids, 1e9), axis=0, keepdims=True)
            state[values] = jnp.where(row == r, m, state[values])
            state[winners] = jnp.where(row == r, idx, state[winners])
            state[scores] = jnp.where(ids == idx, -jnp.inf, s)
        return step

    return [make(r) for r in range(PEER_TOPK)]


def _candidate_tables():
    pos, pen = [], []
    for a, width in [(0, PEER_TOPK)] + [(a, SUBLANES) for a in range(1, SUBLANES)]:
        for b in range(width):
            pos.append(a * PEER_TOPK + b)
            pen.append(0.0 if (a + 1) * (b + 1) <= PEER_TOPK else -np.inf)
    for a in range(SUBLANES, PEER_TOPK):
        pos.append(a * PEER_TOPK)
        pen.append(0.0)
    return (np.asarray(pos, np.float32).reshape(-1, 1), np.asarray(pen, np.float32).reshape(-1, 1))


def _route_steps(hr_ref, wq_ref, sk_ref, pos_ref, pen_ref, h, p, val_scr, idx_scr, ids_ref, gates_ref):
    st = {}
    zeros = jnp.zeros((PEER_TOPK, LANES), F32)

    def start():
        q = jnp.dot(hr_ref[...].astype(BF16), wq_ref[h, :, p * PEER_HALF:(p + 1) * PEER_HALF],
                    preferred_element_type=F32).astype(BF16)
        st["scores"] = lax.dot_general(sk_ref[2 * h + p], q, (((1,), (1,)), ((), ())),
                                       preferred_element_type=F32)
        st["keys"] = lax.broadcasted_iota(jnp.int32, (PEER_KEYS, LANES), 0).astype(F32)
        st["val"], st["idx"] = zeros, zeros

    steps = [start] + _top_round_steps(st, "scores", "keys", "val", "idx")

    if p == 0:
        def finish():
            val_scr[...] = st["val"]
            idx_scr[...] = st["idx"]

        return steps + [finish]

    def start_head():
        v1, v2 = val_scr[...], st["val"]
        pen = jnp.broadcast_to(pen_ref[...], (pen_ref.shape[0], LANES))
        st["cand"] = jnp.concatenate(
            [v1[0:1, :] + v2]
            + [v1[a:a + 1, :] + v2[0:SUBLANES, :] for a in range(1, SUBLANES)]
            + [v1[SUBLANES:, :] + v2[0:1, :]], axis=0) + pen
        st["pos"] = jnp.broadcast_to(pos_ref[...], (pos_ref.shape[0], LANES))
        st["best"], st["win"] = zeros, zeros

    def finish_head():
        i1, i2 = idx_scr[...], st["idx"]
        pos = st["win"].astype(jnp.int32)
        pa = pos // PEER_TOPK
        pb = pos % PEER_TOPK
        e1, e2 = zeros, zeros
        for a in range(PEER_TOPK):
            e1 = e1 + jnp.where(pa == a, i1[a:a + 1, :], 0.0)
            e2 = e2 + jnp.where(pb == a, i2[a:a + 1, :], 0.0)
        best = st["best"]
        ex = jnp.exp(best - best[0:1, :])
        head_rows = pl.ds(pl.multiple_of(h * PEER_TOPK, PEER_TOPK), PEER_TOPK)
        ids_ref[head_rows, :] = (e1 * PEER_KEYS + e2).astype(jnp.int32)
        gates_ref[head_rows, :] = ex / jnp.sum(ex, axis=0, keepdims=True)

    return (steps + [start_head] + _top_round_steps(st, "cand", "pos", "best", "win")
            + [finish_head])


def _block_kernel(x_ref, q_ref, kvc_ref, kvp_ref, pc_ref, pp_ref, bkt_ref, relb_ref, sink_ref,
                  poolw_ref, pscale_ref, wout_ref, g2_ref, wq_ref, sk_ref, pos_ref, pen_ref,
                  tab_hbm, tab_tiles_hbm, out_ref,
                  bias_scr, kv_scr, p_scr, x1_ring, h2_ring,
                  ring, ring_sems, ids_smem, ids_stage, gates_ring, val_scr, idx_scr, ids_sem,
                  *, blocks, blocks_per_seq):
    s = pl.program_id(0)
    d = x_ref.shape[1]
    chunks = d // LANES
    groups = BLOCK // EXPERT_SUB
    turns = groups // EXPERT_RING
    tok_tiles = PEER_SLOTS // SUBLANES
    half = PEER_SLOTS // 2
    gate_slots = gates_ring.shape[0]

    @pl.when(s == 0)
    def _bias():
        _build_bias(bkt_ref, relb_ref, bias_scr)

    mixed_slot = s % MIX_RING
    _mixer_block(x_ref, q_ref, kvc_ref, kvp_ref, pc_ref, pp_ref, sink_ref, poolw_ref, pscale_ref,
                 wout_ref, g2_ref, bias_scr, kv_scr, p_scr,
                 jnp.minimum(s, blocks - 1) % blocks_per_seq,
                 x1_ring.at[mixed_slot], h2_ring.at[mixed_slot])

    routed_h2 = h2_ring.at[(s + MIX_RING - 1) % MIX_RING]
    routed_gates = gates_ring.at[(s + gate_slots - 1) % gate_slots]

    def route_steps(h, p, stage):
        return _route_steps(routed_h2, wq_ref, sk_ref, pos_ref, pen_ref, h, p, val_scr, idx_scr,
                            ids_stage.at[stage], routed_gates)

    def ids_copy(stage, parity):
        col0 = parity * BLOCK
        if not isinstance(col0, int):
            col0 = pl.multiple_of(col0, BLOCK)
        return pltpu.make_async_copy(ids_stage.at[stage], ids_smem.at[:, pl.ds(col0, BLOCK)],
                                     ids_sem.at[0])

    @pl.when((s >= 1) & (s < MIX_AHEAD))
    def _route_only():
        def head(h, carry):
            for p in range(2):
                for step in route_steps(h, p, 0):
                    step()
            return carry

        lax.fori_loop(0, PEER_HEADS, head, 0)
        copy = ids_copy(0, (s + 1) & 1)
        copy.start()
        copy.wait()

    @pl.when(s >= MIX_AHEAD)
    def _experts():
        i = s - MIX_AHEAD
        last = blocks - 1
        h2_blk = h2_ring.at[i % MIX_RING]
        x1_blk = x1_ring.at[i % MIX_RING]

        def issue(group, slot, tt, k0, k1):
            if isinstance(group, int):
                block_off, in_block = divmod(group, groups)
            else:
                block_off, in_block = group >> (groups.bit_length() - 1), group & (groups - 1)
            col = ((i + block_off) & 1) * BLOCK + in_block * EXPERT_SUB + tt
            for k in range(k0, k1):
                row = ids_smem[k, col]
                dst = ring.at[slot, tt * tok_tiles + k // SUBLANES, :, k % SUBLANES]
                pltpu.make_async_copy(tab_hbm.at[row], dst, ring_sems.at[slot]).start(priority=k % 2)

        def wait(slot):
            pltpu.make_async_copy(tab_tiles_hbm.at[pl.ds(0, ring.shape[1])], ring.at[slot],
                                  ring_sems.at[slot]).wait()

        @pl.when(i == 0)
        def _first_rows():
            for a in range(EXPERT_AHEAD):
                for tt in range(EXPERT_SUB):
                    issue(a, a, tt, 0, PEER_SLOTS)

        lane = lax.broadcasted_iota(jnp.int32, (PEER_SLOTS, LANES), 1)
        sub = lax.broadcasted_iota(jnp.int32, (EXPERT_SUB, LANES), 0)

        def group_body(g, slot, side_steps):
            ahead = (slot + EXPERT_AHEAD) % EXPERT_RING
            per_issue = -(-len(side_steps) // (2 * EXPERT_SUB))
            pending = list(side_steps)

            def run_side(n):
                for _ in range(min(n, len(pending))):
                    pending.pop(0)()

            wait(slot)
            tok_rows = pl.ds(pl.multiple_of(g * EXPERT_SUB, EXPERT_SUB), EXPERT_SUB)
            hsub = h2_blk[tok_rows, :]

            def tile(tt, c):
                t0 = tt * tok_tiles
                return ring[slot, t0:t0 + tok_tiles, c].reshape(PEER_SLOTS, LANES)

            def cols(j):
                return slice(j * LANES, (j + 1) * LANES)

            acts = jnp.zeros((PEER_SLOTS, LANES), F32)
            for tt in range(EXPERT_SUB):
                issue(g + EXPERT_AHEAD, ahead, tt, 0, half)
                run_side(per_issue)
                acc = tile(tt, 0) * hsub[tt:tt + 1, cols(0)]
                for c in range(1, chunks):
                    acc = acc + tile(tt, c) * hsub[tt:tt + 1, cols(c)]
                acts = jnp.where(lane == tt, jnp.sum(acc, axis=1, keepdims=True), acts)
            gates = pltpu.roll(gates_ring[i % gate_slots], (BLOCK - g * EXPERT_SUB) % BLOCK, 1)
            coef = gates * _gelu(acts)
            o = [jnp.zeros((EXPERT_SUB, LANES), F32) for _ in range(chunks)]
            for tt in range(EXPERT_SUB):
                issue(g + EXPERT_AHEAD, ahead, tt, half, PEER_SLOTS)
                run_side(per_issue)
                cc = coef[:, tt:tt + 1]
                for c in range(chunks):
                    oc = jnp.sum(tile(tt, chunks + c) * cc, axis=0, keepdims=True)
                    o[c] = jnp.where(sub == tt, oc, o[c])
            run_side(len(pending))
            for c in range(chunks):
                out_ref[tok_rows, cols(c)] = x1_blk[tok_rows, cols(c)] + o[c]

        def ring_turn(q, carry):
            @pl.when((q == turns - 1) & (i > 0))
            def _ids_landed():
                ids_copy(0, 0).wait()

            for r in range(EXPERT_RING):
                side = route_steps(q * (EXPERT_RING // 2) + r // 2, r % 2, i & 1)
                group_body(q * EXPERT_RING + r, r, side)
            return carry

        lax.fori_loop(0, turns, ring_turn, 0)
        ids_copy(i & 1, i & 1).start()

        @pl.when(i == last)
        def _drain():
            ids_copy(0, 0).wait()
            for a in range(EXPERT_AHEAD):
                wait((groups + a) % EXPERT_RING)


def _block_pipeline(x2, q, kv, p, rel_bias, sinks, pool_w, pool_scale, w_out, g2,
                    peer_wq, peer_subkeys, peer_u, peer_v, seq):
    T, D = x2.shape
    blocks = T // BLOCK
    groups = BLOCK // EXPERT_SUB
    sub_rows = EXPERT_SUB * PEER_SLOTS
    row_tiles = 2 * D // LANES
    n_experts = peer_u.shape[0]
    assert groups % EXPERT_RING == 0 and groups == 2 * PEER_HEADS and EXPERT_RING % 2 == 0
    assert groups & (groups - 1) == 0 and BLOCK == LANES and BLOCK == KV_HALO
    assert ROUTE_AHEAD == 2 and blocks >= 1 and n_experts % SUBLANES == 0
    assert seq % BLOCK == 0 and BLOCK % CHUNK == 0 and BLOCK % POOL_HALO == 0
    wq = peer_wq.astype(BF16).reshape(D, PEER_HEADS, 2 * PEER_HALF).transpose(1, 0, 2)
    sk = peer_subkeys.astype(BF16).reshape(2 * PEER_HEADS, PEER_KEYS, PEER_HALF)
    pos, pen = _candidate_tables()
    n_cand = pos.shape[0]
    table = _pack_table(peer_u, peer_v)
    table_tiles = table.reshape(n_experts // SUBLANES, row_tiles, SUBLANES, LANES)

    mixed = lambda s: jnp.minimum(s, blocks - 1)
    cur = lambda c: pl.BlockSpec((BLOCK, c), lambda s: (mixed(s), 0))
    halo = lambda rows, c: pl.BlockSpec(
        (rows, c), lambda s: (jnp.maximum(mixed(s) * (BLOCK // rows) - 1, 0), 0))
    const = lambda *shape: pl.BlockSpec(shape, lambda s: (0,) * len(shape))
    smem = pl.BlockSpec(memory_space=pltpu.SMEM)
    hbm = pl.BlockSpec(memory_space=pl.ANY)
    return pl.pallas_call(
        functools.partial(_block_kernel, blocks=blocks, blocks_per_seq=seq // BLOCK),
        grid=(blocks + MIX_AHEAD,),
        in_specs=[cur(D), cur(D_ATTN), cur(2 * D_KV), halo(KV_HALO, 2 * D_KV),
                  cur(D_POOL), halo(POOL_HALO, D_POOL),
                  const(CHUNK, BAND), smem, smem,
                  const(len(POOL_WINDOWS), POOL_GROUP_DIM, POOL_GROUP_DIM),
                  const(1, D_POOL), const(D, D), const(1, D),
                  const(PEER_HEADS, D, 2 * PEER_HALF), const(2 * PEER_HEADS, PEER_KEYS, PEER_HALF),
                  const(n_cand, 1), const(n_cand, 1),
                  hbm, hbm],
        out_specs=pl.BlockSpec((BLOCK, D), lambda s: (jnp.maximum(s - MIX_AHEAD, 0), 0)),
        out_shape=jax.ShapeDtypeStruct((T, D), F32),
        scratch_shapes=[pltpu.VMEM((ATTN_HEADS, CHUNK, BAND), F32),
                        pltpu.VMEM((KV_HALO + BLOCK, 2 * D_KV), BF16),
                        pltpu.VMEM((POOL_HALO + BLOCK, D_POOL), F32),
                        pltpu.VMEM((MIX_RING, BLOCK, D), F32),
                        pltpu.VMEM((MIX_RING, BLOCK, D), F32),
                        pltpu.VMEM((EXPERT_RING, sub_rows // SUBLANES, row_tiles, SUBLANES, LANES),
                                   F32),
                        pltpu.SemaphoreType.DMA((EXPERT_RING,)),
                        pltpu.SMEM((PEER_SLOTS, 2 * BLOCK), jnp.int32),
                        pltpu.VMEM((2, PEER_SLOTS, BLOCK), jnp.int32),
                        pltpu.VMEM((ROUTE_AHEAD + 1, PEER_SLOTS, BLOCK), F32),
                        pltpu.VMEM((PEER_TOPK, BLOCK), F32),
                        pltpu.VMEM((PEER_TOPK, BLOCK), F32),
                        pltpu.SemaphoreType.DMA((1,))],
        compiler_params=pltpu.CompilerParams(dimension_semantics=("arbitrary",),
                                             vmem_limit_bytes=VMEM_LIMIT),
        name="block_pipeline",
    )(x2, q, kv, kv, p, p, jnp.asarray(_t5_bucket_table()), rel_bias.astype(F32),
      sinks.astype(F32), pool_w.astype(BF16), pool_scale.reshape(1, D_POOL),
      w_out.astype(BF16), g2.reshape(1, D), wq, sk, jnp.asarray(pos), jnp.asarray(pen),
      table, table_tiles)


def kernel(x, norm1_g, w_in, q_norm_g, k_norm_g, attn_sinks, rel_bias, pool_w, pool_scale, w_out,
           norm2_g, peer_wq, peer_subkeys, peer_u, peer_v):
    batch, seq, d_model = x.shape
    depth = norm1_g.shape[0]
    assert (batch * seq) % IN_PROJ_ROWS == 0 and peer_u.shape[1] % PACK_ROWS == 0
    x2 = x.reshape(batch * seq, d_model)
    for l in range(depth):
        q, kv, p = _in_proj(x2, norm1_g[l], w_in[l], q_norm_g[l], k_norm_g[l])
        x2 = _block_pipeline(x2, q, kv, p, rel_bias, attn_sinks[l], pool_w[l], pool_scale[l],
                             w_out[l], norm2_g[l], peer_wq[l], peer_subkeys[l],
                             peer_u[l], peer_v[l], seq)
    return x2.reshape(batch, seq, d_model)
```

```python
import functools
import math

import jax
import jax.numpy as jnp
import numpy as np
from jax import lax
from jax.experimental import pallas as pl
from jax.experimental.pallas import tpu as pltpu

F32 = jnp.float32
BF16 = jnp.bfloat16

CHUNK = 64
ATTN_HEADS = 8
ATTN_KV_HEADS = 2
HEAD_DIM = 64
ATTN_GROUP = ATTN_HEADS // ATTN_KV_HEADS
WIN_CHUNKS = 2
BAND = (WIN_CHUNKS + 1) * CHUNK
D_ATTN = ATTN_HEADS * HEAD_DIM
D_KV = ATTN_KV_HEADS * HEAD_DIM
POOL_WINDOWS = (2, 4, 8, 16)
POOL_GROUP_DIM = 128
D_POOL = len(POOL_WINDOWS) * POOL_GROUP_DIM
REL_BUCKETS = 32
REL_MAX_DIST = 128
PEER_KEYS = 128
PEER_HEADS = 8
PEER_TOPK = 16
PEER_HALF = 128
PEER_SLOTS = PEER_HEADS * PEER_TOPK
EPS = 1e-6
NEG_INF = -1e30

LANES = 128
SUBLANES = 8

IN_PROJ_ROWS = 512
POOL_HALO = 16
KV_HALO = WIN_CHUNKS * CHUNK
PACK_ROWS = 256
BLOCK = 128
EXPERT_SUB = 8
EXPERT_AHEAD = 3
EXPERT_RING = EXPERT_AHEAD + 1
ROUTE_AHEAD = 2
MIX_AHEAD = ROUTE_AHEAD + 1
MIX_RING = MIX_AHEAD + 1
MIXER_QUARTER = BLOCK // (BLOCK // EXPERT_SUB // EXPERT_RING)
VMEM_LIMIT = 58 * 1024 * 1024


def _rms(x, g):
    return x * lax.rsqrt(jnp.mean(x * x, axis=-1, keepdims=True) + EPS) * g


def _segment_sumsq(x, ind):
    sq = x * x
    hi = sq.astype(BF16)
    lo = (sq - hi.astype(F32)).astype(BF16)
    return (jnp.dot(hi, ind, preferred_element_type=F32)
            + jnp.dot(lo, ind, preferred_element_type=F32))


def _in_proj_kernel(x_ref, g1_ref, w_ref, gq_ref, gk_ref, indq_ref, indk_ref,
                    q_ref, kv_ref, p_ref):
    h = _rms(x_ref[...], g1_ref[...])
    y = jnp.dot(h.astype(BF16), w_ref[...], preferred_element_type=F32)
    q = y[:, :D_ATTN]
    k = y[:, D_ATTN:D_ATTN + D_KV]
    v = y[:, D_ATTN + D_KV:D_ATTN + 2 * D_KV]
    qn = q * lax.rsqrt(_segment_sumsq(q, indq_ref[...]) * (1.0 / HEAD_DIM) + EPS) * gq_ref[...]
    kn = k * lax.rsqrt(_segment_sumsq(k, indk_ref[...]) * (1.0 / HEAD_DIM) + EPS) * gk_ref[...]
    q_ref[...] = (qn * (HEAD_DIM ** -0.5)).astype(BF16)
    kv_ref[:, :D_KV] = kn.astype(BF16)
    kv_ref[:, D_KV:] = v.astype(BF16)
    p_ref[...] = y[:, D_ATTN + 2 * D_KV:]


def _in_proj(x2, g1, w_in, gq, gk):
    T, D = x2.shape
    d_in = w_in.shape[1]
    head_of = np.arange(D_ATTN) // HEAD_DIM
    indq = jnp.asarray(head_of[:, None] == head_of[None, :], BF16)
    indk = indq[:D_KV, :D_KV]
    full = lambda r, c: pl.BlockSpec((r, c), lambda i: (0, 0))
    rows = lambda c: pl.BlockSpec((IN_PROJ_ROWS, c), lambda i: (i, 0))
    return pl.pallas_call(
        _in_proj_kernel,
        grid=(T // IN_PROJ_ROWS,),
        in_specs=[rows(D), full(1, D), full(D, d_in), full(1, D_ATTN), full(1, D_KV),
                  full(D_ATTN, D_ATTN), full(D_KV, D_KV)],
        out_specs=[rows(D_ATTN), rows(2 * D_KV), rows(D_POOL)],
        out_shape=[jax.ShapeDtypeStruct((T, D_ATTN), BF16),
                   jax.ShapeDtypeStruct((T, 2 * D_KV), BF16),
                   jax.ShapeDtypeStruct((T, D_POOL), F32)],
        compiler_params=pltpu.CompilerParams(dimension_semantics=("arbitrary",),
                                             vmem_limit_bytes=VMEM_LIMIT),
        name="in_proj",
    )(x2, g1.reshape(1, D), w_in.astype(BF16),
      jnp.tile(gq, ATTN_HEADS).reshape(1, D_ATTN), jnp.tile(gk, ATTN_KV_HEADS).reshape(1, D_KV),
      indq, indk)


def _pack_kernel(u_ref, v_ref, out_ref):
    rows = u_ref.shape[0]
    tiles = u_ref.shape[1] // LANES
    for t, src in enumerate((u_ref, v_ref)):
        for c in range(tiles):
            out_ref[pl.ds(t * tiles + c, rows, stride=2 * tiles), :] = src[:, c * LANES:(c + 1) * LANES]


def _pack_table(peer_u, peer_v):
    n, d = peer_u.shape
    row_tiles = 2 * d // LANES
    packed = pl.pallas_call(
        _pack_kernel,
        grid=(n // PACK_ROWS,),
        in_specs=[pl.BlockSpec((PACK_ROWS, d), lambda i: (i, 0)),
                  pl.BlockSpec((PACK_ROWS, d), lambda i: (i, 0))],
        out_specs=pl.BlockSpec((PACK_ROWS * row_tiles, LANES), lambda i: (i, 0)),
        out_shape=jax.ShapeDtypeStruct((n * row_tiles, LANES), F32),
        compiler_params=pltpu.CompilerParams(dimension_semantics=("arbitrary",),
                                             vmem_limit_bytes=VMEM_LIMIT),
        name="pack_table",
    )(peer_u, peer_v)
    return packed.reshape(n, row_tiles, LANES)


def _t5_bucket_table():
    i = np.arange(CHUNK)[:, None]
    j = np.arange(BAND)[None, :]
    rel = (j - WIN_CHUNKS * CHUNK) - i
    nb = REL_BUCKETS // 2
    max_exact = nb // 2
    base = np.where(rel > 0, nb, 0)
    n = np.abs(rel)
    nf = np.maximum(n, 1).astype(np.float64)
    large = max_exact + (np.log(nf / max_exact) / math.log(REL_MAX_DIST / max_exact)
                         * (nb - max_exact)).astype(np.int32)
    large = np.minimum(large, nb - 1)
    return (base + np.where(n < max_exact, n, large)).astype(np.int32)


def _build_bias(bkt_ref, relb_ref, bias_scr):
    bkt = bkt_ref[...]
    for h in range(ATTN_HEADS):
        acc = jnp.zeros((CHUNK, BAND), F32)
        for bucket in range(REL_BUCKETS):
            acc = jnp.where(bkt == bucket, relb_ref[bucket, h], acc)
        bias_scr[h] = acc


def _mixer_prepare(kvc_ref, kvp_ref, pc_ref, pp_ref, kv_scr, p_scr, d_scr, block_in_seq):
    rows = kvc_ref.shape[0]
    kv_scr[:KV_HALO] = kvp_ref[...]
    kv_scr[KV_HALO:] = kvc_ref[...]
    p_scr[:POOL_HALO] = jnp.where(block_in_seq == 0, 0.0, pp_ref[...])
    p_scr[POOL_HALO:] = pc_ref[...]
    t_seq = block_in_seq * rows + lax.broadcasted_iota(jnp.int32, (rows, 1), 0)
    for g, w in enumerate(POOL_WINDOWS):
        lanes = slice(g * POOL_GROUP_DIM, (g + 1) * POOL_GROUP_DIM)
        cur = p_scr[POOL_HALO:POOL_HALO + rows, lanes]
        total = cur
        for j in range(1, w):
            total = total + p_scr[POOL_HALO - j:POOL_HALO - j + rows, lanes]
        cnt = jnp.minimum(t_seq + 1, w).astype(F32)
        d_scr[:, lanes] = (total / cnt - cur).astype(BF16)


def _mixer_quarter_steps(quarter, x_ref, q_ref, sink_ref, poolw_ref, pscale_ref, wout_ref, g2_ref,
                         bias_scr, kv_scr, d_scr, block_in_seq, x1_out, h2_out):
    rows = MIXER_QUARTER
    per_chunk = CHUNK // rows
    row0 = pl.multiple_of(quarter * rows, rows)
    chunk = quarter >> (per_chunk.bit_length() - 1)
    band = pl.ds(pl.multiple_of(chunk * CHUNK, CHUNK), BAND)
    bias_rows = pl.ds(pl.multiple_of((quarter & (per_chunk - 1)) * rows, rows), rows)
    st = {"scores": [None] * ATTN_HEADS, "probs": [None] * ATTN_HEADS, "out": [None] * ATTN_HEADS}

    def score_step(h):
        def step():
            key_slot = lax.broadcasted_iota(jnp.int32, (rows, BAND), 1) // CHUNK
            first_key_chunk = block_in_seq * (BLOCK // CHUNK) + chunk - WIN_CHUNKS
            kvh = h // ATTN_GROUP
            qh = q_ref[pl.ds(row0, rows), h * HEAD_DIM:(h + 1) * HEAD_DIM]
            kb = kv_scr[band, kvh * HEAD_DIM:(kvh + 1) * HEAD_DIM]
            s = lax.dot_general(qh, kb, (((1,), (1,)), ((), ())), preferred_element_type=F32)
            st["scores"][h] = jnp.where((key_slot + first_key_chunk) >= 0,
                                        s + bias_scr[h, bias_rows, :], NEG_INF)
        return step

    def softmax_step(h):
        def step():
            s = st["scores"][h]
            sink = sink_ref[h]
            m = jnp.maximum(jnp.max(s, axis=-1, keepdims=True), sink)
            e = jnp.exp(s - m)
            denom = jnp.sum(e, axis=-1, keepdims=True) + jnp.exp(sink - m)
            st["probs"][h] = (e / denom).astype(BF16)
        return step

    def pv_step(h):
        def step():
            kvh = h // ATTN_GROUP
            vb = kv_scr[band, D_KV + kvh * HEAD_DIM:D_KV + (kvh + 1) * HEAD_DIM]
            st["out"][h] = jnp.dot(st["probs"][h], vb, preferred_element_type=F32)
        return step

    def pool_step():
        pooled = []
        for g in range(len(POOL_WINDOWS)):
            lanes = slice(g * POOL_GROUP_DIM, (g + 1) * POOL_GROUP_DIM)
            og = jnp.dot(d_scr[pl.ds(row0, rows), lanes], poolw_ref[g], preferred_element_type=F32)
            pooled.append(og * pscale_ref[:, lanes])
        st["pooled"] = jnp.concatenate(pooled, axis=1).astype(BF16)

    def project_step():
        attn = jnp.concatenate(st["out"], axis=1).astype(BF16)
        mixed = (jnp.dot(attn, wout_ref[:D_ATTN], preferred_element_type=F32)
                 + jnp.dot(st["pooled"], wout_ref[D_ATTN:], preferred_element_type=F32))
        x1 = x_ref[pl.ds(row0, rows), :] + mixed
        x1_out[pl.ds(row0, rows), :] = x1
        h2_out[pl.ds(row0, rows), :] = _rms(x1, g2_ref[...])

    return ([score_step(h) for h in range(ATTN_HEADS)],
            [softmax_step(h) for h in range(ATTN_HEADS)],
            [pv_step(h) for h in range(ATTN_HEADS)],
            [pool_step, project_step])


def _gelu(x):
    return 0.5 * x * (1.0 + lax.erf(x * (2.0 ** -0.5)))


def _top_round_steps(state, scores, ident, values, winners):
    row = lax.broadcasted_iota(jnp.int32, (PEER_TOPK, LANES), 0)

    def make(r):
        def step():
            s, ids = state[scores], state[ident]
            m = jnp.max(s, axis=0, keepdims=True)
            idx = jnp.min(jnp.where(s == m, ids, 1e9), axis=0, keepdims=True)
            state[values] = jnp.where(row == r, m, state[values])
            state[winners] = jnp.where(row == r, idx, state[winners])
            state[scores] = jnp.where(ids == idx, -jnp.inf, s)
        return step

    return [make(r) for r in range(PEER_TOPK)]


def _candidate_tables():
    pos, pen = [], []
    for a, width in [(0, PEER_TOPK)] + [(a, SUBLANES) for a in range(1, SUBLANES)]:
        for b in range(width):
            pos.append(a * PEER_TOPK + b)
            pen.append(0.0 if (a + 1) * (b + 1) <= PEER_TOPK else -np.inf)
    for a in range(SUBLANES, PEER_TOPK):
        pos.append(a * PEER_TOPK)
        pen.append(0.0)
    return (np.asarray(pos, np.float32).reshape(-1, 1), np.asarray(pen, np.float32).reshape(-1, 1))


def _route_steps(hr_ref, wq_ref, sk_ref, pos_ref, pen_ref, h, p, val_scr, idx_scr, ids_ref, gates_ref):
    st = {}
    zeros = jnp.zeros((PEER_TOPK, LANES), F32)

    def start():
        q = jnp.dot(hr_ref[...].astype(BF16), wq_ref[h, :, p * PEER_HALF:(p + 1) * PEER_HALF],
                    preferred_element_type=F32).astype(BF16)
        st["scores"] = lax.dot_general(sk_ref[2 * h + p], q, (((1,), (1,)), ((), ())),
                                       preferred_element_type=F32)
        st["keys"] = lax.broadcasted_iota(jnp.int32, (PEER_KEYS, LANES), 0).astype(F32)
        st["val"], st["idx"] = zeros, zeros

    steps = [start] + _top_round_steps(st, "scores", "keys", "val", "idx")

    if p == 0:
        def finish():
            val_scr[...] = st["val"]
            idx_scr[...] = st["idx"]

        return steps + [finish]

    def start_head():
        v1, v2 = val_scr[...], st["val"]
        pen = jnp.broadcast_to(pen_ref[...], (pen_ref.shape[0], LANES))
        st["cand"] = jnp.concatenate(
            [v1[0:1, :] + v2]
            + [v1[a:a + 1, :] + v2[0:SUBLANES, :] for a in range(1, SUBLANES)]
            + [v1[SUBLANES:, :] + v2[0:1, :]], axis=0) + pen
        st["pos"] = jnp.broadcast_to(pos_ref[...], (pos_ref.shape[0], LANES))
        st["best"], st["win"] = zeros, zeros

    def finish_head():
        i1, i2 = idx_scr[...], st["idx"]
        pos = st["win"].astype(jnp.int32)
        pa = pos // PEER_TOPK
        pb = pos % PEER_TOPK
        e1, e2 = zeros, zeros
        for a in range(PEER_TOPK):
            e1 = e1 + jnp.where(pa == a, i1[a:a + 1, :], 0.0)
            e2 = e2 + jnp.where(pb == a, i2[a:a + 1, :], 0.0)
        best = st["best"]
        ex = jnp.exp(best - best[0:1, :])
        head_rows = pl.ds(pl.multiple_of(h * PEER_TOPK, PEER_TOPK), PEER_TOPK)
        ids_ref[head_rows, :] = (e1 * PEER_KEYS + e2).astype(jnp.int32)
        gates_ref[head_rows, :] = ex / jnp.sum(ex, axis=0, keepdims=True)

    return (steps + [start_head] + _top_round_steps(st, "cand", "pos", "best", "win")
            + [finish_head])


def _block_kernel(x_ref, q_ref, kvc_ref, kvp_ref, pc_ref, pp_ref, bkt_ref, relb_ref, sink_ref,
                  poolw_ref, pscale_ref, wout_ref, g2_ref, wq_ref, sk_ref, pos_ref, pen_ref,
                  tab_hbm, tab_tiles_hbm, out_ref,
                  bias_scr, kv_scr, p_scr, d_scr, x1_ring, h2_ring,
                  ring, ring_sems, ids_smem, ids_stage, gates_ring, val_scr, idx_scr, ids_sem,
                  *, blocks, blocks_per_seq):
    s = pl.program_id(0)
    d = x_ref.shape[1]
    chunks = d // LANES
    groups = BLOCK // EXPERT_SUB
    turns = groups // EXPERT_RING
    tok_tiles = PEER_SLOTS // SUBLANES
    half = PEER_SLOTS // 2
    gate_slots = gates_ring.shape[0]

    @pl.when(s == 0)
    def _bias():
        _build_bias(bkt_ref, relb_ref, bias_scr)

    mixed_slot = s % MIX_RING
    block_in_seq = jnp.minimum(s, blocks - 1) % blocks_per_seq
    _mixer_prepare(kvc_ref, kvp_ref, pc_ref, pp_ref, kv_scr, p_scr, d_scr, block_in_seq)

    def mixer_steps(quarter):
        return _mixer_quarter_steps(quarter, x_ref, q_ref, sink_ref, poolw_ref, pscale_ref, wout_ref,
                                    g2_ref, bias_scr, kv_scr, d_scr, block_in_seq,
                                    x1_ring.at[mixed_slot], h2_ring.at[mixed_slot])

    @pl.when(s < MIX_AHEAD)
    def _mix_only():
        def quarter(qr, carry):
            for steps in mixer_steps(qr):
                for step in steps:
                    step()
            return carry

        lax.fori_loop(0, turns, quarter, 0)

    routed_h2 = h2_ring.at[(s + MIX_RING - 1) % MIX_RING]
    routed_gates = gates_ring.at[(s + gate_slots - 1) % gate_slots]

    def route_steps(h, p, stage):
        return _route_steps(routed_h2, wq_ref, sk_ref, pos_ref, pen_ref, h, p, val_scr, idx_scr,
                            ids_stage.at[stage], routed_gates)

    def ids_copy(stage, parity):
        col0 = parity * BLOCK
        if not isinstance(col0, int):
            col0 = pl.multiple_of(col0, BLOCK)
        return pltpu.make_async_copy(ids_stage.at[stage], ids_smem.at[:, pl.ds(col0, BLOCK)],
                                     ids_sem.at[0])

    @pl.when((s >= 1) & (s < MIX_AHEAD))
    def _route_only():
        def head(h, carry):
            for p in range(2):
                for step in route_steps(h, p, 0):
                    step()
            return carry

        lax.fori_loop(0, PEER_HEADS, head, 0)
        copy = ids_copy(0, (s + 1) & 1)
        copy.start()
        copy.wait()

    @pl.when(s >= MIX_AHEAD)
    def _experts():
        i = s - MIX_AHEAD
        last = blocks - 1
        h2_blk = h2_ring.at[i % MIX_RING]
        x1_blk = x1_ring.at[i % MIX_RING]

        def issue(group, slot, tt, k0, k1):
            if isinstance(group, int):
                block_off, in_block = divmod(group, groups)
            else:
                block_off, in_block = group >> (groups.bit_length() - 1), group & (groups - 1)
            col = ((i + block_off) & 1) * BLOCK + in_block * EXPERT_SUB + tt
            for k in range(k0, k1):
                row = ids_smem[k, col]
                dst = ring.at[slot, tt * tok_tiles + k // SUBLANES, :, k % SUBLANES]
                pltpu.make_async_copy(tab_hbm.at[row], dst, ring_sems.at[slot]).start(priority=k % 2)

        def wait(slot):
            pltpu.make_async_copy(tab_tiles_hbm.at[pl.ds(0, ring.shape[1])], ring.at[slot],
                                  ring_sems.at[slot]).wait()

        @pl.when(i == 0)
        def _first_rows():
            for a in range(EXPERT_AHEAD):
                for tt in range(EXPERT_SUB):
                    issue(a, a, tt, 0, PEER_SLOTS)

        lane = lax.broadcasted_iota(jnp.int32, (PEER_SLOTS, LANES), 1)
        sub = lax.broadcasted_iota(jnp.int32, (EXPERT_SUB, LANES), 0)

        def group_body(g, slot, side_steps):
            ahead = (slot + EXPERT_AHEAD) % EXPERT_RING
            per_issue = -(-len(side_steps) // (2 * EXPERT_SUB))
            pending = list(side_steps)

            def run_side(n):
                for _ in range(min(n, len(pending))):
                    pending.pop(0)()

            wait(slot)
            tok_rows = pl.ds(pl.multiple_of(g * EXPERT_SUB, EXPERT_SUB), EXPERT_SUB)
            hsub = h2_blk[tok_rows, :]

            def tile(tt, c):
                t0 = tt * tok_tiles
                return ring[slot, t0:t0 + tok_tiles, c].reshape(PEER_SLOTS, LANES)

            def cols(j):
                return slice(j * LANES, (j + 1) * LANES)

            acts = jnp.zeros((PEER_SLOTS, LANES), F32)
            for tt in range(EXPERT_SUB):
                issue(g + EXPERT_AHEAD, ahead, tt, 0, half)
                run_side(per_issue)
                acc = tile(tt, 0) * hsub[tt:tt + 1, cols(0)]
                for c in range(1, chunks):
                    acc = acc + tile(tt, c) * hsub[tt:tt + 1, cols(c)]
                acts = jnp.where(lane == tt, jnp.sum(acc, axis=1, keepdims=True), acts)
            gates = pltpu.roll(gates_ring[i % gate_slots], (BLOCK - g * EXPERT_SUB) % BLOCK, 1)
            coef = gates * _gelu(acts)
            o = [jnp.zeros((EXPERT_SUB, LANES), F32) for _ in range(chunks)]
            for tt in range(EXPERT_SUB):
                issue(g + EXPERT_AHEAD, ahead, tt, half, PEER_SLOTS)
                run_side(per_issue)
                cc = coef[:, tt:tt + 1]
                for c in range(chunks):
                    oc = jnp.sum(tile(tt, chunks + c) * cc, axis=0, keepdims=True)
                    o[c] = jnp.where(sub == tt, oc, o[c])
            run_side(len(pending))
            for c in range(chunks):
                out_ref[tok_rows, cols(c)] = x1_blk[tok_rows, cols(c)] + o[c]

        def ring_turn(q, carry):
            @pl.when((q == turns - 1) & (i > 0))
            def _ids_landed():
                ids_copy(0, 0).wait()

            mixing = mixer_steps(q)
            for r in range(EXPERT_RING):
                side = route_steps(q * (EXPERT_RING // 2) + r // 2, r % 2, i & 1) + mixing[r]
                group_body(q * EXPERT_RING + r, r, side)
            return carry

        lax.fori_loop(0, turns, ring_turn, 0)
        ids_copy(i & 1, i & 1).start()

        @pl.when(i == last)
        def _drain():
            ids_copy(0, 0).wait()
            for a in range(EXPERT_AHEAD):
                wait((groups + a) % EXPERT_RING)


def _block_pipeline(x2, q, kv, p, rel_bias, sinks, pool_w, pool_scale, w_out, g2,
                    peer_wq, peer_subkeys, peer_u, peer_v, seq):
    T, D = x2.shape
    blocks = T // BLOCK
    groups = BLOCK // EXPERT_SUB
    sub_rows = EXPERT_SUB * PEER_SLOTS
    row_tiles = 2 * D // LANES
    n_experts = peer_u.shape[0]
    assert groups % EXPERT_RING == 0 and groups == 2 * PEER_HEADS and EXPERT_RING % 2 == 0
    assert groups & (groups - 1) == 0 and BLOCK == LANES and BLOCK == KV_HALO
    assert ROUTE_AHEAD == 2 and blocks >= 1 and n_experts % SUBLANES == 0
    assert seq % BLOCK == 0 and BLOCK % CHUNK == 0 and BLOCK % POOL_HALO == 0
    wq = peer_wq.astype(BF16).reshape(D, PEER_HEADS, 2 * PEER_HALF).transpose(1, 0, 2)
    sk = peer_subkeys.astype(BF16).reshape(2 * PEER_HEADS, PEER_KEYS, PEER_HALF)
    pos, pen = _candidate_tables()
    n_cand = pos.shape[0]
    table = _pack_table(peer_u, peer_v)
    table_tiles = table.reshape(n_experts // SUBLANES, row_tiles, SUBLANES, LANES)

    mixed = lambda s: jnp.minimum(s, blocks - 1)
    cur = lambda c: pl.BlockSpec((BLOCK, c), lambda s: (mixed(s), 0))
    halo = lambda rows, c: pl.BlockSpec(
        (rows, c), lambda s: (jnp.maximum(mixed(s) * (BLOCK // rows) - 1, 0), 0))
    const = lambda *shape: pl.BlockSpec(shape, lambda s: (0,) * len(shape))
    smem = pl.BlockSpec(memory_space=pltpu.SMEM)
    hbm = pl.BlockSpec(memory_space=pl.ANY)
    return pl.pallas_call(
        functools.partial(_block_kernel, blocks=blocks, blocks_per_seq=seq // BLOCK),
        grid=(blocks + MIX_AHEAD,),
        in_specs=[cur(D), cur(D_ATTN), cur(2 * D_KV), halo(KV_HALO, 2 * D_KV),
                  cur(D_POOL), halo(POOL_HALO, D_POOL),
                  const(CHUNK, BAND), smem, smem,
                  const(len(POOL_WINDOWS), POOL_GROUP_DIM, POOL_GROUP_DIM),
                  const(1, D_POOL), const(D, D), const(1, D),
                  const(PEER_HEADS, D, 2 * PEER_HALF), const(2 * PEER_HEADS, PEER_KEYS, PEER_HALF),
                  const(n_cand, 1), const(n_cand, 1),
                  hbm, hbm],
        out_specs=pl.BlockSpec((BLOCK, D), lambda s: (jnp.maximum(s - MIX_AHEAD, 0), 0)),
        out_shape=jax.ShapeDtypeStruct((T, D), F32),
        scratch_shapes=[pltpu.VMEM((ATTN_HEADS, CHUNK, BAND), F32),
                        pltpu.VMEM((KV_HALO + BLOCK, 2 * D_KV), BF16),
                        pltpu.VMEM((POOL_HALO + BLOCK, D_POOL), F32),
                        pltpu.VMEM((BLOCK, D_POOL), BF16),
                        pltpu.VMEM((MIX_RING, BLOCK, D), F32),
                        pltpu.VMEM((MIX_RING, BLOCK, D), F32),
                        pltpu.VMEM((EXPERT_RING, sub_rows // SUBLANES, row_tiles, SUBLANES, LANES),
                                   F32),
                        pltpu.SemaphoreType.DMA((EXPERT_RING,)),
                        pltpu.SMEM((PEER_SLOTS, 2 * BLOCK), jnp.int32),
                        pltpu.VMEM((2, PEER_SLOTS, BLOCK), jnp.int32),
                        pltpu.VMEM((ROUTE_AHEAD + 1, PEER_SLOTS, BLOCK), F32),
                        pltpu.VMEM((PEER_TOPK, BLOCK), F32),
                        pltpu.VMEM((PEER_TOPK, BLOCK), F32),
                        pltpu.SemaphoreType.DMA((1,))],
        compiler_params=pltpu.CompilerParams(dimension_semantics=("arbitrary",),
                                             vmem_limit_bytes=VMEM_LIMIT),
        name="block_pipeline",
    )(x2, q, kv, kv, p, p, jnp.asarray(_t5_bucket_table()), rel_bias.astype(F32),
      sinks.astype(F32), pool_w.astype(BF16), pool_scale.reshape(1, D_POOL),
      w_out.astype(BF16), g2.reshape(1, D), wq, sk, jnp.asarray(pos), jnp.asarray(pen),
      table, table_tiles)


def kernel(x, norm1_g, w_in, q_norm_g, k_norm_g, attn_sinks, rel_bias, pool_w, pool_scale, w_out,
           norm2_g, peer_wq, peer_subkeys, peer_u, peer_v):
    batch, seq, d_model = x.shape
    depth = norm1_g.shape[0]
    assert (batch * seq) % IN_PROJ_ROWS == 0 and peer_u.shape[1] % PACK_ROWS == 0
    x2 = x.reshape(batch * seq, d_model)
    for l in range(depth):
        q, kv, p = _in_proj(x2, norm1_g[l], w_in[l], q_norm_g[l], k_norm_g[l])
        x2 = _block_pipeline(x2, q, kv, p, rel_bias, attn_sinks[l], pool_w[l], pool_scale[l],
                             w_out[l], norm2_g[l], peer_wq[l], peer_subkeys[l],
                             peer_u[l], peer_v[l], seq)
    return x2.reshape(batch, seq, d_model)
```

```python
import functools
import math

import jax
import jax.numpy as jnp
import numpy as np
from jax import lax
from jax.experimental import pallas as pl
from jax.experimental.pallas import tpu as pltpu

F32 = jnp.float32
BF16 = jnp.bfloat16

CHUNK = 64
ATTN_HEADS = 8
ATTN_KV_HEADS = 2
HEAD_DIM = 64
ATTN_GROUP = ATTN_HEADS // ATTN_KV_HEADS
WIN_CHUNKS = 2
BAND = (WIN_CHUNKS + 1) * CHUNK
D_ATTN = ATTN_HEADS * HEAD_DIM
D_KV = ATTN_KV_HEADS * HEAD_DIM
POOL_WINDOWS = (2, 4, 8, 16)
POOL_GROUP_DIM = 128
D_POOL = len(POOL_WINDOWS) * POOL_GROUP_DIM
REL_BUCKETS = 32
REL_MAX_DIST = 128
PEER_KEYS = 128
PEER_HEADS = 8
PEER_TOPK = 16
PEER_HALF = 128
PEER_SLOTS = PEER_HEADS * PEER_TOPK
EPS = 1e-6
NEG_INF = -1e30

LANES = 128
SUBLANES = 8

IN_PROJ_ROWS = 512
POOL_HALO = 16
KV_HALO = WIN_CHUNKS * CHUNK
BLOCK = 128
EXPERT_SUB = 8
EXPERT_AHEAD = 3
EXPERT_RING = EXPERT_AHEAD + 1
ROUTE_AHEAD = 2
MIX_AHEAD = ROUTE_AHEAD + 1
MIX_RING = MIX_AHEAD + 1
MIXER_QUARTER = BLOCK // (BLOCK // EXPERT_SUB // EXPERT_RING)
VMEM_LIMIT = 58 * 1024 * 1024


def _rms(x, g):
    return x * lax.rsqrt(jnp.mean(x * x, axis=-1, keepdims=True) + EPS) * g


def _segment_sumsq(x, ind):
    sq = x * x
    hi = sq.astype(BF16)
    lo = (sq - hi.astype(F32)).astype(BF16)
    return (jnp.dot(hi, ind, preferred_element_type=F32)
            + jnp.dot(lo, ind, preferred_element_type=F32))


def _in_proj_kernel(x_ref, g1_ref, w_ref, gq_ref, gk_ref, indq_ref, indk_ref, u_ref, v_ref,
                    q_ref, kv_ref, p_ref, tab_ref):
    experts = u_ref.shape[0]
    tiles = u_ref.shape[1] // LANES
    for t, src in enumerate((u_ref, v_ref)):
        for c in range(tiles):
            tab_ref[pl.ds(t * tiles + c, experts, stride=2 * tiles), :] = src[:, c * LANES:(c + 1) * LANES]

    h = _rms(x_ref[...], g1_ref[...])
    y = jnp.dot(h.astype(BF16), w_ref[...], preferred_element_type=F32)
    q = y[:, :D_ATTN]
    k = y[:, D_ATTN:D_ATTN + D_KV]
    v = y[:, D_ATTN + D_KV:D_ATTN + 2 * D_KV]
    qn = q * lax.rsqrt(_segment_sumsq(q, indq_ref[...]) * (1.0 / HEAD_DIM) + EPS) * gq_ref[...]
    kn = k * lax.rsqrt(_segment_sumsq(k, indk_ref[...]) * (1.0 / HEAD_DIM) + EPS) * gk_ref[...]
    q_ref[...] = (qn * (HEAD_DIM ** -0.5)).astype(BF16)
    kv_ref[:, :D_KV] = kn.astype(BF16)
    kv_ref[:, D_KV:] = v.astype(BF16)
    p_ref[...] = y[:, D_ATTN + 2 * D_KV:]


def _in_proj(x2, g1, w_in, gq, gk, peer_u, peer_v):
    T, D = x2.shape
    d_in = w_in.shape[1]
    steps = T // IN_PROJ_ROWS
    n_experts, d_exp = peer_u.shape
    per_step = n_experts // steps
    assert n_experts % steps == 0 and per_step % SUBLANES == 0
    row_tiles = 2 * d_exp // LANES
    head_of = np.arange(D_ATTN) // HEAD_DIM
    indq = jnp.asarray(head_of[:, None] == head_of[None, :], BF16)
    indk = indq[:D_KV, :D_KV]
    full = lambda r, c: pl.BlockSpec((r, c), lambda i: (0, 0))
    rows = lambda c: pl.BlockSpec((IN_PROJ_ROWS, c), lambda i: (i, 0))
    exps = lambda r, c: pl.BlockSpec((r, c), lambda i: (i, 0))
    q, kv, p, table = pl.pallas_call(
        _in_proj_kernel,
        grid=(steps,),
        in_specs=[rows(D), full(1, D), full(D, d_in), full(1, D_ATTN), full(1, D_KV),
                  full(D_ATTN, D_ATTN), full(D_KV, D_KV),
                  exps(per_step, d_exp), exps(per_step, d_exp)],
        out_specs=[rows(D_ATTN), rows(2 * D_KV), rows(D_POOL), exps(per_step * row_tiles, LANES)],
        out_shape=[jax.ShapeDtypeStruct((T, D_ATTN), BF16),
                   jax.ShapeDtypeStruct((T, 2 * D_KV), BF16),
                   jax.ShapeDtypeStruct((T, D_POOL), F32),
                   jax.ShapeDtypeStruct((n_experts * row_tiles, LANES), F32)],
        compiler_params=pltpu.CompilerParams(dimension_semantics=("arbitrary",),
                                             vmem_limit_bytes=VMEM_LIMIT),
        name="in_proj",
    )(x2, g1.reshape(1, D), w_in.astype(BF16),
      jnp.tile(gq, ATTN_HEADS).reshape(1, D_ATTN), jnp.tile(gk, ATTN_KV_HEADS).reshape(1, D_KV),
      indq, indk, peer_u, peer_v)
    return q, kv, p, table.reshape(n_experts, row_tiles, LANES)


def _t5_bucket_table():
    i = np.arange(CHUNK)[:, None]
    j = np.arange(BAND)[None, :]
    rel = (j - WIN_CHUNKS * CHUNK) - i
    nb = REL_BUCKETS // 2
    max_exact = nb // 2
    base = np.where(rel > 0, nb, 0)
    n = np.abs(rel)
    nf = np.maximum(n, 1).astype(np.float64)
    large = max_exact + (np.log(nf / max_exact) / math.log(REL_MAX_DIST / max_exact)
                         * (nb - max_exact)).astype(np.int32)
    large = np.minimum(large, nb - 1)
    return (base + np.where(n < max_exact, n, large)).astype(np.int32)


def _build_bias(bkt_ref, relb_ref, bias_scr):
    bkt = bkt_ref[...]
    for h in range(ATTN_HEADS):
        acc = jnp.zeros((CHUNK, BAND), F32)
        for bucket in range(REL_BUCKETS):
            acc = jnp.where(bkt == bucket, relb_ref[bucket, h], acc)
        bias_scr[h] = acc


def _mixer_prepare(kvc_ref, kvp_ref, pc_ref, pp_ref, kv_scr, p_scr, d_scr, block_in_seq):
    rows = kvc_ref.shape[0]
    kv_scr[:KV_HALO] = kvp_ref[...]
    kv_scr[KV_HALO:] = kvc_ref[...]
    p_scr[:POOL_HALO] = jnp.where(block_in_seq == 0, 0.0, pp_ref[...])
    p_scr[POOL_HALO:] = pc_ref[...]
    t_seq = block_in_seq * rows + lax.broadcasted_iota(jnp.int32, (rows, 1), 0)
    for g, w in enumerate(POOL_WINDOWS):
        lanes = slice(g * POOL_GROUP_DIM, (g + 1) * POOL_GROUP_DIM)
        cur = p_scr[POOL_HALO:POOL_HALO + rows, lanes]
        total = cur
        for j in range(1, w):
            total = total + p_scr[POOL_HALO - j:POOL_HALO - j + rows, lanes]
        cnt = jnp.minimum(t_seq + 1, w).astype(F32)
        d_scr[:, lanes] = (total / cnt - cur).astype(BF16)


def _mixer_quarter_steps(quarter, x_ref, q_ref, sink_ref, poolw_ref, pscale_ref, wout_ref, g2_ref,
                         bias_scr, kv_scr, d_scr, block_in_seq, x1_out, h2_out):
    rows = MIXER_QUARTER
    per_chunk = CHUNK // rows
    row0 = pl.multiple_of(quarter * rows, rows)
    chunk = quarter >> (per_chunk.bit_length() - 1)
    band = pl.ds(pl.multiple_of(chunk * CHUNK, CHUNK), BAND)
    bias_rows = pl.ds(pl.multiple_of((quarter & (per_chunk - 1)) * rows, rows), rows)
    st = {"scores": [None] * ATTN_HEADS, "probs": [None] * ATTN_HEADS, "out": [None] * ATTN_HEADS}

    def score_step(h):
        def step():
            key_slot = lax.broadcasted_iota(jnp.int32, (rows, BAND), 1) // CHUNK
            first_key_chunk = block_in_seq * (BLOCK // CHUNK) + chunk - WIN_CHUNKS
            kvh = h // ATTN_GROUP
            qh = q_ref[pl.ds(row0, rows), h * HEAD_DIM:(h + 1) * HEAD_DIM]
            kb = kv_scr[band, kvh * HEAD_DIM:(kvh + 1) * HEAD_DIM]
            s = lax.dot_general(qh, kb, (((1,), (1,)), ((), ())), preferred_element_type=F32)
            st["scores"][h] = jnp.where((key_slot + first_key_chunk) >= 0,
                                        s + bias_scr[h, bias_rows, :], NEG_INF)
        return step

    def softmax_step(h):
        def step():
            s = st["scores"][h]
            sink = sink_ref[h]
            m = jnp.maximum(jnp.max(s, axis=-1, keepdims=True), sink)
            e = jnp.exp(s - m)
            denom = jnp.sum(e, axis=-1, keepdims=True) + jnp.exp(sink - m)
            st["probs"][h] = (e / denom).astype(BF16)
        return step

    def pv_step(h):
        def step():
            kvh = h // ATTN_GROUP
            vb = kv_scr[band, D_KV + kvh * HEAD_DIM:D_KV + (kvh + 1) * HEAD_DIM]
            st["out"][h] = jnp.dot(st["probs"][h], vb, preferred_element_type=F32)
        return step

    def pool_step():
        pooled = []
        for g in range(len(POOL_WINDOWS)):
            lanes = slice(g * POOL_GROUP_DIM, (g + 1) * POOL_GROUP_DIM)
            og = jnp.dot(d_scr[pl.ds(row0, rows), lanes], poolw_ref[g], preferred_element_type=F32)
            pooled.append(og * pscale_ref[:, lanes])
        st["pooled"] = jnp.concatenate(pooled, axis=1).astype(BF16)

    def project_step():
        attn = jnp.concatenate(st["out"], axis=1).astype(BF16)
        mixed = (jnp.dot(attn, wout_ref[:D_ATTN], preferred_element_type=F32)
                 + jnp.dot(st["pooled"], wout_ref[D_ATTN:], preferred_element_type=F32))
        x1 = x_ref[pl.ds(row0, rows), :] + mixed
        x1_out[pl.ds(row0, rows), :] = x1
        h2_out[pl.ds(row0, rows), :] = _rms(x1, g2_ref[...])

    return ([score_step(h) for h in range(ATTN_HEADS)],
            [softmax_step(h) for h in range(ATTN_HEADS)],
            [pv_step(h) for h in range(ATTN_HEADS)],
            [pool_step, project_step])


def _gelu(x):
    return 0.5 * x * (1.0 + lax.erf(x * (2.0 ** -0.5)))


def _top_round_steps(state, scores, ident, values, winners):
    row = lax.broadcasted_iota(jnp.int32, (PEER_TOPK, LANES), 0)

    def make(r):
        def step():
            s, ids = state[scores], state[ident]
            m = jnp.max(s, axis=0, keepdims=True)
            idx = jnp.min(jnp.where(s == m, ids, 1e9), axis=0, keepdims=True)
            state[values] = jnp.where(row == r, m, state[values])
            state[winners] = jnp.where(row == r, idx, state[winners])
            state[scores] = jnp.where(ids == idx, -jnp.inf, s)
        return step

    return [make(r) for r in range(PEER_TOPK)]


def _candidate_tables():
    pos, pen = [], []
    for a, width in [(0, PEER_TOPK)] + [(a, SUBLANES) for a in range(1, SUBLANES)]:
        for b in range(width):
            pos.append(a * PEER_TOPK + b)
            pen.append(0.0 if (a + 1) * (b + 1) <= PEER_TOPK else -np.inf)
    for a in range(SUBLANES, PEER_TOPK):
        pos.append(a * PEER_TOPK)
        pen.append(0.0)
    return (np.asarray(pos, np.float32).reshape(-1, 1), np.asarray(pen, np.float32).reshape(-1, 1))


def _route_steps(hr_ref, wq_ref, sk_ref, pos_ref, pen_ref, h, p, val_scr, idx_scr, ids_ref, gates_ref):
    st = {}
    zeros = jnp.zeros((PEER_TOPK, LANES), F32)

    def start():
        q = jnp.dot(hr_ref[...].astype(BF16), wq_ref[h, :, p * PEER_HALF:(p + 1) * PEER_HALF],
                    preferred_element_type=F32).astype(BF16)
        st["scores"] = lax.dot_general(sk_ref[2 * h + p], q, (((1,), (1,)), ((), ())),
                                       preferred_element_type=F32)
        st["keys"] = lax.broadcasted_iota(jnp.int32, (PEER_KEYS, LANES), 0).astype(F32)
        st["val"], st["idx"] = zeros, zeros

    steps = [start] + _top_round_steps(st, "scores", "keys", "val", "idx")

    if p == 0:
        def finish():
            val_scr[...] = st["val"]
            idx_scr[...] = st["idx"]

        return steps + [finish]

    def start_head():
        v1, v2 = val_scr[...], st["val"]
        pen = jnp.broadcast_to(pen_ref[...], (pen_ref.shape[0], LANES))
        st["cand"] = jnp.concatenate(
            [v1[0:1, :] + v2]
            + [v1[a:a + 1, :] + v2[0:SUBLANES, :] for a in range(1, SUBLANES)]
            + [v1[SUBLANES:, :] + v2[0:1, :]], axis=0) + pen
        st["pos"] = jnp.broadcast_to(pos_ref[...], (pos_ref.shape[0], LANES))
        st["best"], st["win"] = zeros, zeros

    def finish_head():
        i1, i2 = idx_scr[...], st["idx"]
        pos = st["win"].astype(jnp.int32)
        pa = pos // PEER_TOPK
        pb = pos % PEER_TOPK
        e1, e2 = zeros, zeros
        for a in range(PEER_TOPK):
            e1 = e1 + jnp.where(pa == a, i1[a:a + 1, :], 0.0)
            e2 = e2 + jnp.where(pb == a, i2[a:a + 1, :], 0.0)
        best = st["best"]
        ex = jnp.exp(best - best[0:1, :])
        head_rows = pl.ds(pl.multiple_of(h * PEER_TOPK, PEER_TOPK), PEER_TOPK)
        ids_ref[head_rows, :] = (e1 * PEER_KEYS + e2).astype(jnp.int32)
        gates_ref[head_rows, :] = ex / jnp.sum(ex, axis=0, keepdims=True)

    return (steps + [start_head] + _top_round_steps(st, "cand", "pos", "best", "win")
            + [finish_head])


def _block_kernel(x_ref, q_ref, kvc_ref, kvp_ref, pc_ref, pp_ref, bkt_ref, relb_ref, sink_ref,
                  poolw_ref, pscale_ref, wout_ref, g2_ref, wq_ref, sk_ref, pos_ref, pen_ref,
                  tab_hbm, tab_tiles_hbm, out_ref,
                  bias_scr, kv_scr, p_scr, d_scr, x1_ring, h2_ring,
                  ring, ring_sems, ids_smem, ids_stage, gates_ring, val_scr, idx_scr, ids_sem,
                  *, blocks, blocks_per_seq):
    s = pl.program_id(0)
    d = x_ref.shape[1]
    chunks = d // LANES
    groups = BLOCK // EXPERT_SUB
    turns = groups // EXPERT_RING
    tok_tiles = PEER_SLOTS // SUBLANES
    half = PEER_SLOTS // 2
    gate_slots = gates_ring.shape[0]

    @pl.when(s == 0)
    def _bias():
        _build_bias(bkt_ref, relb_ref, bias_scr)

    mixed_slot = s % MIX_RING
    block_in_seq = jnp.minimum(s, blocks - 1) % blocks_per_seq
    _mixer_prepare(kvc_ref, kvp_ref, pc_ref, pp_ref, kv_scr, p_scr, d_scr, block_in_seq)

    def mixer_steps(quarter):
        return _mixer_quarter_steps(quarter, x_ref, q_ref, sink_ref, poolw_ref, pscale_ref, wout_ref,
                                    g2_ref, bias_scr, kv_scr, d_scr, block_in_seq,
                                    x1_ring.at[mixed_slot], h2_ring.at[mixed_slot])

    @pl.when(s < MIX_AHEAD)
    def _mix_only():
        def quarter(qr, carry):
            for steps in mixer_steps(qr):
                for step in steps:
                    step()
            return carry

        lax.fori_loop(0, turns, quarter, 0)

    routed_h2 = h2_ring.at[(s + MIX_RING - 1) % MIX_RING]
    routed_gates = gates_ring.at[(s + gate_slots - 1) % gate_slots]

    def route_steps(h, p, stage):
        return _route_steps(routed_h2, wq_ref, sk_ref, pos_ref, pen_ref, h, p, val_scr, idx_scr,
                            ids_stage.at[stage], routed_gates)

    def ids_copy(stage, parity):
        col0 = parity * BLOCK
        if not isinstance(col0, int):
            col0 = pl.multiple_of(col0, BLOCK)
        return pltpu.make_async_copy(ids_stage.at[stage], ids_smem.at[:, pl.ds(col0, BLOCK)],
                                     ids_sem.at[0])

    @pl.when((s >= 1) & (s < MIX_AHEAD))
    def _route_only():
        def head(h, carry):
            for p in range(2):
                for step in route_steps(h, p, 0):
                    step()
            return carry

        lax.fori_loop(0, PEER_HEADS, head, 0)
        copy = ids_copy(0, (s + 1) & 1)
        copy.start()
        copy.wait()

    @pl.when(s >= MIX_AHEAD)
    def _experts():
        i = s - MIX_AHEAD
        last = blocks - 1
        h2_blk = h2_ring.at[i % MIX_RING]
        x1_blk = x1_ring.at[i % MIX_RING]

        def issue(group, slot, tt, k0, k1):
            if isinstance(group, int):
                block_off, in_block = divmod(group, groups)
            else:
                block_off, in_block = group >> (groups.bit_length() - 1), group & (groups - 1)
            col = ((i + block_off) & 1) * BLOCK + in_block * EXPERT_SUB + tt
            for k in range(k0, k1):
                row = ids_smem[k, col]
                dst = ring.at[slot, tt * tok_tiles + k // SUBLANES, :, k % SUBLANES]
                pltpu.make_async_copy(tab_hbm.at[row], dst, ring_sems.at[slot]).start(priority=k % 2)

        def wait(slot):
            pltpu.make_async_copy(tab_tiles_hbm.at[pl.ds(0, ring.shape[1])], ring.at[slot],
                                  ring_sems.at[slot]).wait()

        @pl.when(i == 0)
        def _first_rows():
            for a in range(EXPERT_AHEAD):
                for tt in range(EXPERT_SUB):
                    issue(a, a, tt, 0, PEER_SLOTS)

        lane = lax.broadcasted_iota(jnp.int32, (PEER_SLOTS, LANES), 1)
        sub = lax.broadcasted_iota(jnp.int32, (EXPERT_SUB, LANES), 0)

        def group_body(g, slot, side_steps):
            ahead = (slot + EXPERT_AHEAD) % EXPERT_RING
            per_issue = -(-len(side_steps) // (2 * EXPERT_SUB))
            pending = list(side_steps)

            def run_side(n):
                for _ in range(min(n, len(pending))):
                    pending.pop(0)()

            wait(slot)
            tok_rows = pl.ds(pl.multiple_of(g * EXPERT_SUB, EXPERT_SUB), EXPERT_SUB)
            hsub = h2_blk[tok_rows, :]

            def tile(tt, c):
                t0 = tt * tok_tiles
                return ring[slot, t0:t0 + tok_tiles, c].reshape(PEER_SLOTS, LANES)

            def cols(j):
                return slice(j * LANES, (j + 1) * LANES)

            acts = jnp.zeros((PEER_SLOTS, LANES), F32)
            for tt in range(EXPERT_SUB):
                issue(g + EXPERT_AHEAD, ahead, tt, 0, half)
                run_side(per_issue)
                acc = tile(tt, 0) * hsub[tt:tt + 1, cols(0)]
                for c in range(1, chunks):
                    acc = acc + tile(tt, c) * hsub[tt:tt + 1, cols(c)]
                acts = jnp.where(lane == tt, jnp.sum(acc, axis=1, keepdims=True), acts)
            gates = pltpu.roll(gates_ring[i % gate_slots], (BLOCK - g * EXPERT_SUB) % BLOCK, 1)
            coef = gates * _gelu(acts)
            o = [jnp.zeros((EXPERT_SUB, LANES), F32) for _ in range(chunks)]
            for tt in range(EXPERT_SUB):
                issue(g + EXPERT_AHEAD, ahead, tt, half, PEER_SLOTS)
                run_side(per_issue)
                cc = coef[:, tt:tt + 1]
                for c in range(chunks):
                    oc = jnp.sum(tile(tt, chunks + c) * cc, axis=0, keepdims=True)
                    o[c] = jnp.where(sub == tt, oc, o[c])
            run_side(len(pending))
            for c in range(chunks):
                out_ref[tok_rows, cols(c)] = x1_blk[tok_rows, cols(c)] + o[c]

        def ring_turn(q, carry):
            @pl.when((q == turns - 1) & (i > 0))
            def _ids_landed():
                ids_copy(0, 0).wait()

            mixing = mixer_steps(q)
            for r in range(EXPERT_RING):
                side = route_steps(q * (EXPERT_RING // 2) + r // 2, r % 2, i & 1) + mixing[r]
                group_body(q * EXPERT_RING + r, r, side)
            return carry

        lax.fori_loop(0, turns, ring_turn, 0)
        ids_copy(i & 1, i & 1).start()

        @pl.when(i == last)
        def _drain():
            ids_copy(0, 0).wait()
            for a in range(EXPERT_AHEAD):
                wait((groups + a) % EXPERT_RING)


def _block_pipeline(x2, q, kv, p, rel_bias, sinks, pool_w, pool_scale, w_out, g2,
                    peer_wq, peer_subkeys, table, seq):
    T, D = x2.shape
    blocks = T // BLOCK
    groups = BLOCK // EXPERT_SUB
    sub_rows = EXPERT_SUB * PEER_SLOTS
    n_experts, row_tiles, _ = table.shape
    assert row_tiles == 2 * D // LANES
    assert groups % EXPERT_RING == 0 and groups == 2 * PEER_HEADS and EXPERT_RING % 2 == 0
    assert groups & (groups - 1) == 0 and BLOCK == LANES and BLOCK == KV_HALO
    assert ROUTE_AHEAD == 2 and blocks >= 1 and n_experts % SUBLANES == 0
    assert seq % BLOCK == 0 and BLOCK % CHUNK == 0 and BLOCK % POOL_HALO == 0
    wq = peer_wq.astype(BF16).reshape(D, PEER_HEADS, 2 * PEER_HALF).transpose(1, 0, 2)
    sk = peer_subkeys.astype(BF16).reshape(2 * PEER_HEADS, PEER_KEYS, PEER_HALF)
    pos, pen = _candidate_tables()
    n_cand = pos.shape[0]
    table_tiles =table.reshape(n_experts // SUBLANES, row_tiles, SUBLANES, LANES)

    mixed = lambda s: jnp.minimum(s, blocks - 1)
    cur = lambda c: pl.BlockSpec((BLOCK, c), lambda s: (mixed(s), 0))
    halo = lambda rows, c: pl.BlockSpec(
        (rows, c), lambda s: (jnp.maximum(mixed(s) * (BLOCK // rows) - 1, 0), 0))
    const = lambda *shape: pl.BlockSpec(shape, lambda s: (0,) * len(shape))
    smem = pl.BlockSpec(memory_space=pltpu.SMEM)
    hbm = pl.BlockSpec(memory_space=pl.ANY)
    return pl.pallas_call(
        functools.partial(_block_kernel, blocks=blocks, blocks_per_seq=seq // BLOCK),
        grid=(blocks + MIX_AHEAD,),
        in_specs=[cur(D), cur(D_ATTN), cur(2 * D_KV), halo(KV_HALO, 2 * D_KV),
                  cur(D_POOL), halo(POOL_HALO, D_POOL),
                  const(CHUNK, BAND), smem, smem,
                  const(len(POOL_WINDOWS), POOL_GROUP_DIM, POOL_GROUP_DIM),
                  const(1, D_POOL), const(D, D), const(1, D),
                  const(PEER_HEADS, D, 2 * PEER_HALF), const(2 * PEER_HEADS, PEER_KEYS, PEER_HALF),
                  const(n_cand, 1), const(n_cand, 1),
                  hbm, hbm],
        out_specs=pl.BlockSpec((BLOCK, D), lambda s: (jnp.maximum(s - MIX_AHEAD, 0), 0)),
        out_shape=jax.ShapeDtypeStruct((T, D), F32),
        scratch_shapes=[pltpu.VMEM((ATTN_HEADS, CHUNK, BAND), F32),
                        pltpu.VMEM((KV_HALO + BLOCK, 2 * D_KV), BF16),
                        pltpu.VMEM((POOL_HALO + BLOCK, D_POOL), F32),
                        pltpu.VMEM((BLOCK, D_POOL), BF16),
                        pltpu.VMEM((MIX_RING, BLOCK, D), F32),
                        pltpu.VMEM((MIX_RING, BLOCK, D), F32),
                        pltpu.VMEM((EXPERT_RING, sub_rows // SUBLANES, row_tiles, SUBLANES, LANES),
                                   F32),
                        pltpu.SemaphoreType.DMA((EXPERT_RING,)),
                        pltpu.SMEM((PEER_SLOTS, 2 * BLOCK), jnp.int32),
                        pltpu.VMEM((2, PEER_SLOTS, BLOCK), jnp.int32),
                        pltpu.VMEM((ROUTE_AHEAD + 1, PEER_SLOTS, BLOCK), F32),
                        pltpu.VMEM((PEER_TOPK, BLOCK), F32),
                        pltpu.VMEM((PEER_TOPK, BLOCK), F32),
                        pltpu.SemaphoreType.DMA((1,))],
        compiler_params=pltpu.CompilerParams(dimension_semantics=("arbitrary",),
                                             vmem_limit_bytes=VMEM_LIMIT),
        name="block_pipeline",
    )(x2, q, kv, kv, p, p, jnp.asarray(_t5_bucket_table()), rel_bias.astype(F32),
      sinks.astype(F32), pool_w.astype(BF16), pool_scale.reshape(1, D_POOL),
      w_out.astype(BF16), g2.reshape(1, D), wq, sk, jnp.asarray(pos), jnp.asarray(pen),
      table, table_tiles)


def kernel(x, norm1_g, w_in, q_norm_g, k_norm_g, attn_sinks, rel_bias, pool_w, pool_scale, w_out,
           norm2_g, peer_wq, peer_subkeys, peer_u, peer_v):
    batch, seq, d_model = x.shape
    depth = norm1_g.shape[0]
    assert (batch * seq) % IN_PROJ_ROWS == 0
    x2 = x.reshape(batch * seq, d_model)
    for l in range(depth):
        q, kv, p, table = _in_proj(x2, norm1_g[l], w_in[l], q_norm_g[l], k_norm_g[l],
                                   peer_u[l], peer_v[l])
        x2 = _block_pipeline(x2, q, kv, p, rel_bias, attn_sinks[l], pool_w[l], pool_scale[l],
                             w_out[l], norm2_g[l], peer_wq[l], peer_subkeys[l], table, seq)
    return x2.reshape(batch, seq, d_model)
```

```python
import functools
import math

import jax
import jax.numpy as jnp
import numpy as np
from jax import lax
from jax.experimental import pallas as pl
from jax.experimental.pallas import tpu as pltpu

F32 = jnp.float32
BF16 = jnp.bfloat16

CHUNK = 64
ATTN_HEADS = 8
ATTN_KV_HEADS = 2
HEAD_DIM = 64
ATTN_GROUP = ATTN_HEADS // ATTN_KV_HEADS
WIN_CHUNKS = 2
BAND = (WIN_CHUNKS + 1) * CHUNK
D_ATTN = ATTN_HEADS * HEAD_DIM
D_KV = ATTN_KV_HEADS * HEAD_DIM
POOL_WINDOWS = (2, 4, 8, 16)
POOL_GROUP_DIM = 128
D_POOL = len(POOL_WINDOWS) * POOL_GROUP_DIM
REL_BUCKETS = 32
REL_MAX_DIST = 128
PEER_KEYS = 128
PEER_HEADS = 8
PEER_TOPK = 16
PEER_HALF = 128
PEER_SLOTS = PEER_HEADS * PEER_TOPK
EPS = 1e-6
NEG_INF = -1e30

LANES = 128
SUBLANES = 8

IN_PROJ_ROWS = 512
POOL_HALO = 16
KV_HALO = WIN_CHUNKS * CHUNK
BLOCK = 128
EXPERT_SUB = 8
EXPERT_AHEAD = 3
EXPERT_RING = EXPERT_AHEAD + 1
ROUTE_AHEAD = 2
MIX_AHEAD = ROUTE_AHEAD + 1
MIX_RING = MIX_AHEAD + 1
MIXER_QUARTER = BLOCK // (BLOCK // EXPERT_SUB // EXPERT_RING)
VMEM_LIMIT = 58 * 1024 * 1024


def _rms(x, g):
    return x * lax.rsqrt(jnp.mean(x * x, axis=-1, keepdims=True) + EPS) * g


def _segment_sumsq(x, ind):
    sq = x * x
    hi = sq.astype(BF16)
    lo = (sq - hi.astype(F32)).astype(BF16)
    return (jnp.dot(hi, ind, preferred_element_type=F32)
            + jnp.dot(lo, ind, preferred_element_type=F32))


def _in_proj_kernel(x_ref, g1_ref, w_ref, gq_ref, gk_ref, indq_ref, indk_ref, u_ref, v_ref,
                    q_ref, kv_ref, p_ref, tab_ref):
    experts = u_ref.shape[0]
    tiles = u_ref.shape[1] // LANES
    for t, src in enumerate((u_ref, v_ref)):
        for c in range(tiles):
            tab_ref[pl.ds(t * tiles + c, experts, stride=2 * tiles), :] = src[:, c * LANES:(c + 1) * LANES]

    h = _rms(x_ref[...], g1_ref[...])
    y = jnp.dot(h.astype(BF16), w_ref[...], preferred_element_type=F32)
    q = y[:, :D_ATTN]
    k = y[:, D_ATTN:D_ATTN + D_KV]
    v = y[:, D_ATTN + D_KV:D_ATTN + 2 * D_KV]
    qn = q * lax.rsqrt(_segment_sumsq(q, indq_ref[...]) * (1.0 / HEAD_DIM) + EPS) * gq_ref[...]
    kn = k * lax.rsqrt(_segment_sumsq(k, indk_ref[...]) * (1.0 / HEAD_DIM) + EPS) * gk_ref[...]
    q_ref[...] = (qn * (HEAD_DIM ** -0.5)).astype(BF16)
    kv_ref[:, :D_KV] = kn.astype(BF16)
    kv_ref[:, D_KV:] = v.astype(BF16)
    p_ref[...] = y[:, D_ATTN + 2 * D_KV:]


def _in_proj(x2, g1, w_in, gq, gk, peer_u, peer_v):
    T, D = x2.shape
    d_in = w_in.shape[1]
    steps = T // IN_PROJ_ROWS
    n_experts, d_exp = peer_u.shape
    per_step = n_experts // steps
    assert n_experts % steps == 0 and per_step % SUBLANES == 0
    row_tiles = 2 * d_exp // LANES
    head_of = np.arange(D_ATTN) // HEAD_DIM
    indq = jnp.asarray(head_of[:, None] == head_of[None, :], BF16)
    indk = indq[:D_KV, :D_KV]
    full = lambda r, c: pl.BlockSpec((r, c), lambda i: (0, 0))
    rows = lambda c: pl.BlockSpec((IN_PROJ_ROWS, c), lambda i: (i, 0))
    exps = lambda r, c: pl.BlockSpec((r, c), lambda i: (i, 0))
    q, kv, p, table = pl.pallas_call(
        _in_proj_kernel,
        grid=(steps,),
        in_specs=[rows(D), full(1, D), full(D, d_in), full(1, D_ATTN), full(1, D_KV),
                  full(D_ATTN, D_ATTN), full(D_KV, D_KV),
                  exps(per_step, d_exp), exps(per_step, d_exp)],
        out_specs=[rows(D_ATTN), rows(2 * D_KV), rows(D_POOL), exps(per_step * row_tiles, LANES)],
        out_shape=[jax.ShapeDtypeStruct((T, D_ATTN), BF16),
                   jax.ShapeDtypeStruct((T, 2 * D_KV), BF16),
                   jax.ShapeDtypeStruct((T, D_POOL), F32),
                   jax.ShapeDtypeStruct((n_experts * row_tiles, LANES), F32)],
        compiler_params=pltpu.CompilerParams(dimension_semantics=("arbitrary",),
                                             vmem_limit_bytes=VMEM_LIMIT),
        name="in_proj",
    )(x2, g1.reshape(1, D), w_in.astype(BF16),
      jnp.tile(gq, ATTN_HEADS).reshape(1, D_ATTN), jnp.tile(gk, ATTN_KV_HEADS).reshape(1, D_KV),
      indq, indk, peer_u, peer_v)
    return q, kv, p, table.reshape(n_experts, row_tiles, LANES)


def _t5_bucket_table():
    i = np.arange(CHUNK)[:, None]
    j = np.arange(BAND)[None, :]
    rel = (j - WIN_CHUNKS * CHUNK) - i
    nb = REL_BUCKETS // 2
    max_exact = nb // 2
    base = np.where(rel > 0, nb, 0)
    n = np.abs(rel)
    nf = np.maximum(n, 1).astype(np.float64)
    large = max_exact + (np.log(nf / max_exact) / math.log(REL_MAX_DIST / max_exact)
                         * (nb - max_exact)).astype(np.int32)
    large = np.minimum(large, nb - 1)
    return (base + np.where(n < max_exact, n, large)).astype(np.int32)


def _build_bias(bkt_ref, relb_ref, bias_scr):
    bkt = bkt_ref[...]
    for h in range(ATTN_HEADS):
        acc = jnp.zeros((CHUNK, BAND), F32)
        for bucket in range(REL_BUCKETS):
            acc = jnp.where(bkt == bucket, relb_ref[bucket, h], acc)
        bias_scr[h] = acc


def _mixer_prepare(kvc_ref, kvp_ref, pc_ref, pp_ref, kv_scr, p_scr, d_scr, block_in_seq):
    rows = kvc_ref.shape[0]
    kv_scr[:KV_HALO] = kvp_ref[...]
    kv_scr[KV_HALO:] = kvc_ref[...]
    p_scr[:POOL_HALO] = jnp.where(block_in_seq == 0, 0.0, pp_ref[...])
    p_scr[POOL_HALO:] = pc_ref[...]
    t_seq = block_in_seq * rows + lax.broadcasted_iota(jnp.int32, (rows, 1), 0)
    for g, w in enumerate(POOL_WINDOWS):
        lanes = slice(g * POOL_GROUP_DIM, (g + 1) * POOL_GROUP_DIM)
        cur = p_scr[POOL_HALO:POOL_HALO + rows, lanes]
        total = cur
        for j in range(1, w):
            total = total + p_scr[POOL_HALO - j:POOL_HALO - j + rows, lanes]
        cnt = jnp.minimum(t_seq + 1, w).astype(F32)
        d_scr[:, lanes] = (total / cnt - cur).astype(BF16)


def _mixer_quarter_steps(quarter, x_ref, q_ref, sink_ref, poolw_ref, pscale_ref, wout_ref, g2_ref,
                         bias_scr, kv_scr, d_scr, block_in_seq, x1_out, h2_out):
    rows = MIXER_QUARTER
    per_chunk = CHUNK // rows
    row0 = pl.multiple_of(quarter * rows, rows)
    chunk = quarter >> (per_chunk.bit_length() - 1)
    band = pl.ds(pl.multiple_of(chunk * CHUNK, CHUNK), BAND)
    bias_rows = pl.ds(pl.multiple_of((quarter & (per_chunk - 1)) * rows, rows), rows)
    st = {"scores": [None] * ATTN_HEADS, "probs": [None] * ATTN_HEADS, "out": [None] * ATTN_HEADS}

    def score_step(h):
        def step():
            key_slot = lax.broadcasted_iota(jnp.int32, (rows, BAND), 1) // CHUNK
            first_key_chunk = block_in_seq * (BLOCK // CHUNK) + chunk - WIN_CHUNKS
            kvh = h // ATTN_GROUP
            qh = q_ref[pl.ds(row0, rows), h * HEAD_DIM:(h + 1) * HEAD_DIM]
            kb = kv_scr[band, kvh * HEAD_DIM:(kvh + 1) * HEAD_DIM]
            s = lax.dot_general(qh, kb, (((1,), (1,)), ((), ())), preferred_element_type=F32)
            st["scores"][h] = jnp.where((key_slot + first_key_chunk) >= 0,
                                        s + bias_scr[h, bias_rows, :], NEG_INF)
        return step

    def softmax_step(h):
        def step():
            s = st["scores"][h]
            sink = sink_ref[h]
            m = jnp.maximum(jnp.max(s, axis=-1, keepdims=True), sink)
            e = jnp.exp(s - m)
            denom = jnp.sum(e, axis=-1, keepdims=True) + jnp.exp(sink - m)
            st["probs"][h] = (e / denom).astype(BF16)
        return step

    def pv_step(h):
        def step():
            kvh = h // ATTN_GROUP
            vb = kv_scr[band, D_KV + kvh * HEAD_DIM:D_KV + (kvh + 1) * HEAD_DIM]
            st["out"][h] = jnp.dot(st["probs"][h], vb, preferred_element_type=F32)
        return step

    def pool_step():
        pooled = []
        for g in range(len(POOL_WINDOWS)):
            lanes = slice(g * POOL_GROUP_DIM, (g + 1) * POOL_GROUP_DIM)
            og = jnp.dot(d_scr[pl.ds(row0, rows), lanes], poolw_ref[g], preferred_element_type=F32)
            pooled.append(og * pscale_ref[:, lanes])
        st["pooled"] = jnp.concatenate(pooled, axis=1).astype(BF16)

    def project_step():
        attn = jnp.concatenate(st["out"], axis=1).astype(BF16)
        mixed = (jnp.dot(attn, wout_ref[:D_ATTN], preferred_element_type=F32)
                 + jnp.dot(st["pooled"], wout_ref[D_ATTN:], preferred_element_type=F32))
        x1 = x_ref[pl.ds(row0, rows), :] + mixed
        x1_out[pl.ds(row0, rows), :] = x1
        h2_out[pl.ds(row0, rows), :] = _rms(x1, g2_ref[...])

    return ([score_step(h) for h in range(ATTN_HEADS)],
            [softmax_step(h) for h in range(ATTN_HEADS)],
            [pv_step(h) for h in range(ATTN_HEADS)],
            [pool_step, project_step])


def _gelu(x):
    return 0.5 * x * (1.0 + lax.erf(x * (2.0 ** -0.5)))


def _top_round_steps(state, scores, ident, values, winners):
    row = lax.broadcasted_iota(jnp.int32, (PEER_TOPK, LANES), 0)

    def make(r):
        def step():
            s, ids = state[scores], state[ident]
            m = jnp.max(s, axis=0, keepdims=True)
            idx = jnp.min(jnp.where(s == m, ids, 1e9), axis=0, keepdims=True)
            state[values] = jnp.where(row == r, m, state[values])
            state[winners] = jnp.where(row == r, idx, state[winners])
            state[scores] = jnp.where(ids == idx, -jnp.inf, s)
        return step

    return [make(r) for r in range(PEER_TOPK)]


def _candidate_tables():
    pos, pen = [], []
    for a, width in [(0, PEER_TOPK)] + [(a, SUBLANES) for a in range(1, SUBLANES)]:
        for b in range(width):
            pos.append(a * PEER_TOPK + b)
            pen.append(0.0 if (a + 1) * (b + 1) <= PEER_TOPK else -np.inf)
    for a in range(SUBLANES, PEER_TOPK):
        pos.append(a * PEER_TOPK)
        pen.append(0.0)
    return (np.asarray(pos, np.float32).reshape(-1, 1), np.asarray(pen, np.float32).reshape(-1, 1))


def _route_steps(hr_ref, wq_ref, sk_ref, pos_ref, pen_ref, h, p, val_scr, idx_scr, ids_ref, gates_ref):
    st = {}
    zeros = jnp.zeros((PEER_TOPK, LANES), F32)

    def start():
        q = jnp.dot(hr_ref[...].astype(BF16), wq_ref[h, :, p * PEER_HALF:(p + 1) * PEER_HALF],
                    preferred_element_type=F32).astype(BF16)
        st["scores"] = lax.dot_general(sk_ref[2 * h + p], q, (((1,), (1,)), ((), ())),
                                       preferred_element_type=F32)
        st["keys"] = lax.broadcasted_iota(jnp.int32, (PEER_KEYS, LANES), 0).astype(F32)
        st["val"], st["idx"] = zeros, zeros

    steps = [start] + _top_round_steps(st, "scores", "keys", "val", "idx")

    if p == 0:
        def finish():
            val_scr[...] = st["val"]
            idx_scr[...] = st["idx"]

        return steps + [finish]

    def start_head():
        v1, v2 = val_scr[...], st["val"]
        pen = jnp.broadcast_to(pen_ref[...], (pen_ref.shape[0], LANES))
        st["cand"] = jnp.concatenate(
            [v1[0:1, :] + v2]
            + [v1[a:a + 1, :] + v2[0:SUBLANES, :] for a in range(1, SUBLANES)]
            + [v1[SUBLANES:, :] + v2[0:1, :]], axis=0) + pen
        st["pos"] = jnp.broadcast_to(pos_ref[...], (pos_ref.shape[0], LANES))
        st["best"], st["win"] = zeros, zeros

    def finish_head():
        i1, i2 = idx_scr[...], st["idx"]
        pos = st["win"].astype(jnp.int32)
        pa = pos // PEER_TOPK
        pb = pos % PEER_TOPK
        e1, e2 = zeros, zeros
        for a in range(PEER_TOPK):
            e1 = e1 + jnp.where(pa == a, i1[a:a + 1, :], 0.0)
            e2 = e2 + jnp.where(pb == a, i2[a:a + 1, :], 0.0)
        best = st["best"]
        ex = jnp.exp(best - best[0:1, :])
        head_rows = pl.ds(pl.multiple_of(h * PEER_TOPK, PEER_TOPK), PEER_TOPK)
        ids_ref[head_rows, :] = (e1 * PEER_KEYS + e2).astype(jnp.int32)
        gates_ref[head_rows, :] = ex / jnp.sum(ex, axis=0, keepdims=True)

    return (steps + [start_head] + _top_round_steps(st, "cand", "pos", "best", "win")
            + [finish_head])


def _block_kernel(x_ref, q_ref, kvc_ref, kvp_ref, pc_ref, pp_ref, bkt_ref, relb_ref, sink_ref,
                  poolw_ref, pscale_ref, wout_ref, g2_ref, wq_ref, sk_ref, pos_ref, pen_ref,
                  tab_hbm, tab_slabs_hbm, out_ref,
                  bias_scr, kv_scr, p_scr, d_scr, x1_ring, h2_ring,
                  ring, ring_sems, ids_smem, ids_stage, gates_ring, val_scr, idx_scr, ids_sem,
                  *, blocks, blocks_per_seq):
    s = pl.program_id(0)
    d = x_ref.shape[1]
    chunks = d // LANES
    groups = BLOCK // EXPERT_SUB
    turns = groups // EXPERT_RING
    row_tiles = tab_hbm.shape[1]
    pitch = row_tiles + 1
    half = PEER_SLOTS // 2
    gate_slots = gates_ring.shape[0]

    @pl.when(s == 0)
    def _bias():
        _build_bias(bkt_ref, relb_ref, bias_scr)

    mixed_slot = s % MIX_RING
    block_in_seq = jnp.minimum(s, blocks - 1) % blocks_per_seq
    _mixer_prepare(kvc_ref, kvp_ref, pc_ref, pp_ref, kv_scr, p_scr, d_scr, block_in_seq)

    def mixer_steps(quarter):
        return _mixer_quarter_steps(quarter, x_ref, q_ref, sink_ref, poolw_ref, pscale_ref, wout_ref,
                                    g2_ref, bias_scr, kv_scr, d_scr, block_in_seq,
                                    x1_ring.at[mixed_slot], h2_ring.at[mixed_slot])

    @pl.when(s < MIX_AHEAD)
    def _mix_only():
        def quarter(qr, carry):
            for steps in mixer_steps(qr):
                for step in steps:
                    step()
            return carry

        lax.fori_loop(0, turns, quarter, 0)

    routed_h2 = h2_ring.at[(s + MIX_RING - 1) % MIX_RING]
    routed_gates = gates_ring.at[(s + gate_slots - 1) % gate_slots]

    def route_steps(h, p, stage):
        return _route_steps(routed_h2, wq_ref, sk_ref, pos_ref, pen_ref, h, p, val_scr, idx_scr,
                            ids_stage.at[stage], routed_gates)

    def ids_copy(stage, parity):
        col0 = parity * BLOCK
        if not isinstance(col0, int):
            col0 = pl.multiple_of(col0, BLOCK)
        return pltpu.make_async_copy(ids_stage.at[stage], ids_smem.at[:, pl.ds(col0, BLOCK)],
                                     ids_sem.at[0])

    @pl.when((s >= 1) & (s < MIX_AHEAD))
    def _route_only():
        def head(h, carry):
            for p in range(2):
                for step in route_steps(h, p, 0):
                    step()
            return carry

        lax.fori_loop(0, PEER_HEADS, head, 0)
        copy = ids_copy(0, (s + 1) & 1)
        copy.start()
        copy.wait()

    @pl.when(s >= MIX_AHEAD)
    def _experts():
        i = s - MIX_AHEAD
        last = blocks - 1
        h2_blk = h2_ring.at[i % MIX_RING]
        x1_blk = x1_ring.at[i % MIX_RING]

        def issue(group, slot, tt, k0, k1):
            if isinstance(group, int):
                block_off, in_block = divmod(group, groups)
            else:
                block_off, in_block = group >> (groups.bit_length() - 1), group & (groups - 1)
            col = ((i + block_off) & 1) * BLOCK + in_block * EXPERT_SUB + tt
            for k in range(k0, k1):
                row = ids_smem[k, col]
                dst = ring.at[slot, pl.ds((tt * PEER_SLOTS + k) * pitch, row_tiles)]
                pltpu.make_async_copy(tab_hbm.at[row], dst, ring_sems.at[slot]).start(priority=k % 2)

        def wait(slot):
            rows_moved = EXPERT_SUB * PEER_SLOTS * row_tiles
            pltpu.make_async_copy(tab_slabs_hbm.at[pl.ds(0, rows_moved)],
                                  ring.at[slot, pl.ds(0, rows_moved)], ring_sems.at[slot]).wait()

        @pl.when(i == 0)
        def _first_rows():
            for a in range(EXPERT_AHEAD):
                for tt in range(EXPERT_SUB):
                    issue(a, a, tt, 0, PEER_SLOTS)

        lane = lax.broadcasted_iota(jnp.int32, (PEER_SLOTS, LANES), 1)
        sub = lax.broadcasted_iota(jnp.int32, (EXPERT_SUB, LANES), 0)

        def group_body(g, slot, side_steps):
            ahead = (slot + EXPERT_AHEAD) % EXPERT_RING
            per_issue = -(-len(side_steps) // (2 * EXPERT_SUB))
            pending = list(side_steps)

            def run_side(n):
                for _ in range(min(n, len(pending))):
                    pending.pop(0)()

            wait(slot)
            tok_rows = pl.ds(pl.multiple_of(g * EXPERT_SUB, EXPERT_SUB), EXPERT_SUB)
            hsub = h2_blk[tok_rows, :]

            def tile(tt, c):
                return ring[slot, pl.ds(tt * PEER_SLOTS * pitch + c, PEER_SLOTS, stride=pitch), :]

            def cols(j):
                return slice(j * LANES, (j + 1) * LANES)

            acts = jnp.zeros((PEER_SLOTS, LANES), F32)
            for tt in range(EXPERT_SUB):
                issue(g + EXPERT_AHEAD, ahead, tt, 0, half)
                run_side(per_issue)
                acc = tile(tt, 0) * hsub[tt:tt + 1, cols(0)]
                for c in range(1, chunks):
                    acc = acc + tile(tt, c) * hsub[tt:tt + 1, cols(c)]
                acts = jnp.where(lane == tt, jnp.sum(acc, axis=1, keepdims=True), acts)
            gates = pltpu.roll(gates_ring[i % gate_slots], (BLOCK - g * EXPERT_SUB) % BLOCK, 1)
            coef = gates * _gelu(acts)
            o = [jnp.zeros((EXPERT_SUB, LANES), F32) for _ in range(chunks)]
            for tt in range(EXPERT_SUB):
                issue(g + EXPERT_AHEAD, ahead, tt, half, PEER_SLOTS)
                run_side(per_issue)
                cc = coef[:, tt:tt + 1]
                for c in range(chunks):
                    oc = jnp.sum(tile(tt, chunks + c) * cc, axis=0, keepdims=True)
                    o[c] = jnp.where(sub == tt, oc, o[c])
            run_side(len(pending))
            for c in range(chunks):
                out_ref[tok_rows, cols(c)] = x1_blk[tok_rows, cols(c)] + o[c]

        def ring_turn(q, carry):
            @pl.when((q == turns - 1) & (i > 0))
            def _ids_landed():
                ids_copy(0, 0).wait()

            mixing = mixer_steps(q)
            for r in range(EXPERT_RING):
                side = route_steps(q * (EXPERT_RING // 2) + r // 2, r % 2, i & 1) + mixing[r]
                group_body(q * EXPERT_RING + r, r, side)
            return carry

        lax.fori_loop(0, turns, ring_turn, 0)
        ids_copy(i & 1, i & 1).start()

        @pl.when(i == last)
        def _drain():
            ids_copy(0, 0).wait()
            for a in range(EXPERT_AHEAD):
                wait((groups + a) % EXPERT_RING)


def _block_pipeline(x2, q, kv, p, rel_bias, sinks, pool_w, pool_scale, w_out, g2,
                    peer_wq, peer_subkeys, table, seq):
    T, D = x2.shape
    blocks = T // BLOCK
    groups = BLOCK // EXPERT_SUB
    sub_rows = EXPERT_SUB * PEER_SLOTS
    n_experts, row_tiles, _ = table.shape
    assert row_tiles == 2 * D // LANES
    assert groups % EXPERT_RING == 0 and groups == 2 * PEER_HEADS and EXPERT_RING % 2 == 0
    assert groups & (groups - 1) == 0 and BLOCK == LANES and BLOCK == KV_HALO
    assert ROUTE_AHEAD == 2 and blocks >= 1 and n_experts % SUBLANES == 0
    assert seq % BLOCK == 0 and BLOCK % CHUNK == 0 and BLOCK % POOL_HALO == 0
    wq = peer_wq.astype(BF16).reshape(D, PEER_HEADS, 2 * PEER_HALF).transpose(1, 0, 2)
    sk = peer_subkeys.astype(BF16).reshape(2 * PEER_HEADS, PEER_KEYS, PEER_HALF)
    pos, pen = _candidate_tables()
    n_cand = pos.shape[0]
    table_slabs = table.reshape(n_experts * row_tiles, LANES)

    mixed = lambda s: jnp.minimum(s, blocks - 1)
    cur = lambda c: pl.BlockSpec((BLOCK, c), lambda s: (mixed(s), 0))
    halo = lambda rows, c: pl.BlockSpec(
        (rows, c), lambda s: (jnp.maximum(mixed(s) * (BLOCK // rows) - 1, 0), 0))
    const = lambda *shape: pl.BlockSpec(shape, lambda s: (0,) * len(shape))
    smem = pl.BlockSpec(memory_space=pltpu.SMEM)
    hbm = pl.BlockSpec(memory_space=pl.ANY)
    return pl.pallas_call(
        functools.partial(_block_kernel, blocks=blocks, blocks_per_seq=seq // BLOCK),
        grid=(blocks + MIX_AHEAD,),
        in_specs=[cur(D), cur(D_ATTN), cur(2 * D_KV), halo(KV_HALO, 2 * D_KV),
                  cur(D_POOL), halo(POOL_HALO, D_POOL),
                  const(CHUNK, BAND), smem, smem,
                  const(len(POOL_WINDOWS), POOL_GROUP_DIM, POOL_GROUP_DIM),
                  const(1, D_POOL), const(D, D), const(1, D),
                  const(PEER_HEADS, D, 2 * PEER_HALF), const(2 * PEER_HEADS, PEER_KEYS, PEER_HALF),
                  const(n_cand, 1), const(n_cand, 1),
                  hbm, hbm],
        out_specs=pl.BlockSpec((BLOCK, D), lambda s: (jnp.maximum(s - MIX_AHEAD, 0), 0)),
        out_shape=jax.ShapeDtypeStruct((T, D), F32),
        scratch_shapes=[pltpu.VMEM((ATTN_HEADS, CHUNK, BAND), F32),
                        pltpu.VMEM((KV_HALO + BLOCK, 2 * D_KV), BF16),
                        pltpu.VMEM((POOL_HALO + BLOCK, D_POOL), F32),
                        pltpu.VMEM((BLOCK, D_POOL), BF16),
                        pltpu.VMEM((MIX_RING, BLOCK, D), F32),
                        pltpu.VMEM((MIX_RING, BLOCK, D), F32),
                        pltpu.VMEM((EXPERT_RING, sub_rows * (row_tiles + 1), LANES), F32),
                        pltpu.SemaphoreType.DMA((EXPERT_RING,)),
                        pltpu.SMEM((PEER_SLOTS, 2 * BLOCK), jnp.int32),
                        pltpu.VMEM((2, PEER_SLOTS, BLOCK), jnp.int32),
                        pltpu.VMEM((ROUTE_AHEAD + 1, PEER_SLOTS, BLOCK), F32),
                        pltpu.VMEM((PEER_TOPK, BLOCK), F32),
                        pltpu.VMEM((PEER_TOPK, BLOCK), F32),
                        pltpu.SemaphoreType.DMA((1,))],
        compiler_params=pltpu.CompilerParams(dimension_semantics=("arbitrary",),
                                             vmem_limit_bytes=VMEM_LIMIT),
        name="block_pipeline",
    )(x2, q, kv, kv, p, p, jnp.asarray(_t5_bucket_table()), rel_bias.astype(F32),
      sinks.astype(F32), pool_w.astype(BF16), pool_scale.reshape(1, D_POOL),
      w_out.astype(BF16), g2.reshape(1, D), wq, sk, jnp.asarray(pos), jnp.asarray(pen),
      table, table_slabs)


def kernel(x, norm1_g, w_in, q_norm_g, k_norm_g, attn_sinks, rel_bias, pool_w, pool_scale, w_out,
           norm2_g, peer_wq, peer_subkeys, peer_u, peer_v):
    batch, seq, d_model = x.shape
    depth = norm1_g.shape[0]
    assert (batch * seq) % IN_PROJ_ROWS == 0
    x2 = x.reshape(batch * seq, d_model)
    for l in range(depth):
        q, kv, p, table = _in_proj(x2, norm1_g[l], w_in[l], q_norm_g[l], k_norm_g[l],
                                   peer_u[l], peer_v[l])
        x2 = _block_pipeline(x2, q, kv, p, rel_bias, attn_sinks[l], pool_w[l], pool_scale[l],
                             w_out[l], norm2_g[l], peer_wq[l], peer_subkeys[l], table, seq)
    return x2.reshape(batch, seq, d_model)
```

```python
import functools
import math

import jax
import jax.numpy as jnp
import numpy as np
from jax import lax
from jax.experimental import pallas as pl
from jax.experimental.pallas import tpu as pltpu

F32 = jnp.float32
BF16 = jnp.bfloat16

CHUNK = 64
ATTN_HEADS = 8
ATTN_KV_HEADS = 2
HEAD_DIM = 64
ATTN_GROUP = ATTN_HEADS // ATTN_KV_HEADS
WIN_CHUNKS = 2
BAND = (WIN_CHUNKS + 1) * CHUNK
D_ATTN = ATTN_HEADS * HEAD_DIM
D_KV = ATTN_KV_HEADS * HEAD_DIM
POOL_WINDOWS = (2, 4, 8, 16)
POOL_GROUP_DIM = 128
D_POOL = len(POOL_WINDOWS) * POOL_GROUP_DIM
REL_BUCKETS = 32
REL_MAX_DIST = 128
PEER_KEYS = 128
PEER_HEADS = 8
PEER_TOPK = 16
PEER_HALF = 128
PEER_SLOTS = PEER_HEADS * PEER_TOPK
EPS = 1e-6
NEG_INF = -1e30

LANES = 128
SUBLANES = 8

IN_PROJ_ROWS = 512
POOL_HALO = 16
KV_HALO = WIN_CHUNKS * CHUNK
BLOCK = 128
EXPERT_SUB = 8
EXPERT_AHEAD = 3
EXPERT_RING = EXPERT_AHEAD + 1
ROUTE_AHEAD = 2
MIX_AHEAD = ROUTE_AHEAD + 1
MIX_RING = MIX_AHEAD + 1
MIXER_QUARTER = BLOCK // (BLOCK // EXPERT_SUB // EXPERT_RING)
VMEM_LIMIT = 58 * 1024 * 1024


def _rms(x, g):
    return x * lax.rsqrt(jnp.mean(x * x, axis=-1, keepdims=True) + EPS) * g


def _segment_sumsq(x, ind):
    sq = x * x
    hi = sq.astype(BF16)
    lo = (sq - hi.astype(F32)).astype(BF16)
    return (jnp.dot(hi, ind, preferred_element_type=F32)
            + jnp.dot(lo, ind, preferred_element_type=F32))


def _in_proj_kernel(x_ref, g1_ref, w_ref, gq_ref, gk_ref, indq_ref, indk_ref, u_ref, v_ref,
                    q_ref, kv_ref, p_ref, tab_ref):
    experts = u_ref.shape[0]
    tiles = u_ref.shape[1] // LANES
    for t, src in enumerate((u_ref, v_ref)):
        for c in range(tiles):
            tab_ref[pl.ds(t * tiles + c, experts, stride=2 * tiles), :] = src[:, c * LANES:(c + 1) * LANES]

    h = _rms(x_ref[...], g1_ref[...])
    y = jnp.dot(h.astype(BF16), w_ref[...], preferred_element_type=F32)
    q = y[:, :D_ATTN]
    k = y[:, D_ATTN:D_ATTN + D_KV]
    v = y[:, D_ATTN + D_KV:D_ATTN + 2 * D_KV]
    qn = q * lax.rsqrt(_segment_sumsq(q, indq_ref[...]) * (1.0 / HEAD_DIM) + EPS) * gq_ref[...]
    kn = k * lax.rsqrt(_segment_sumsq(k, indk_ref[...]) * (1.0 / HEAD_DIM) + EPS) * gk_ref[...]
    q_ref[...] = (qn * (HEAD_DIM ** -0.5)).astype(BF16)
    kv_ref[:, :D_KV] = kn.astype(BF16)
    kv_ref[:, D_KV:] = v.astype(BF16)
    p_ref[...] = y[:, D_ATTN + 2 * D_KV:]


def _in_proj(x2, g1, w_in, gq, gk, peer_u, peer_v):
    T, D = x2.shape
    d_in = w_in.shape[1]
    steps = T // IN_PROJ_ROWS
    n_experts, d_exp = peer_u.shape
    per_step = n_experts // steps
    assert n_experts % steps == 0 and per_step % SUBLANES == 0
    row_tiles = 2 * d_exp // LANES
    head_of = np.arange(D_ATTN) // HEAD_DIM
    indq = jnp.asarray(head_of[:, None] == head_of[None, :], BF16)
    indk = indq[:D_KV, :D_KV]
    full = lambda r, c: pl.BlockSpec((r, c), lambda i: (0, 0))
    rows = lambda c: pl.BlockSpec((IN_PROJ_ROWS, c), lambda i: (i, 0))
    exps = lambda r, c: pl.BlockSpec((r, c), lambda i: (i, 0))
    q, kv, p, table = pl.pallas_call(
        _in_proj_kernel,
        grid=(steps,),
        in_specs=[rows(D), full(1, D), full(D, d_in), full(1, D_ATTN), full(1, D_KV),
                  full(D_ATTN, D_ATTN), full(D_KV, D_KV),
                  exps(per_step, d_exp), exps(per_step, d_exp)],
        out_specs=[rows(D_ATTN), rows(2 * D_KV), rows(D_POOL), exps(per_step * row_tiles, LANES)],
        out_shape=[jax.ShapeDtypeStruct((T, D_ATTN), BF16),
                   jax.ShapeDtypeStruct((T, 2 * D_KV), BF16),
                   jax.ShapeDtypeStruct((T, D_POOL), F32),
                   jax.ShapeDtypeStruct((n_experts * row_tiles, LANES), F32)],
        compiler_params=pltpu.CompilerParams(dimension_semantics=("arbitrary",),
                                             vmem_limit_bytes=VMEM_LIMIT),
        name="in_proj",
    )(x2, g1.reshape(1, D), w_in.astype(BF16),
      jnp.tile(gq, ATTN_HEADS).reshape(1, D_ATTN), jnp.tile(gk, ATTN_KV_HEADS).reshape(1, D_KV),
      indq, indk, peer_u, peer_v)
    return q, kv, p, table.reshape(n_experts, row_tiles, LANES)


def _t5_bucket_table():
    i = np.arange(CHUNK)[:, None]
    j = np.arange(BAND)[None, :]
    rel = (j - WIN_CHUNKS * CHUNK) - i
    nb = REL_BUCKETS // 2
    max_exact = nb // 2
    base = np.where(rel > 0, nb, 0)
    n = np.abs(rel)
    nf = np.maximum(n, 1).astype(np.float64)
    large = max_exact + (np.log(nf / max_exact) / math.log(REL_MAX_DIST / max_exact)
                         * (nb - max_exact)).astype(np.int32)
    large = np.minimum(large, nb - 1)
    return (base + np.where(n < max_exact, n, large)).astype(np.int32)


def _build_bias(bkt_ref, relb_ref, bias_scr):
    bkt = bkt_ref[...]
    for h in range(ATTN_HEADS):
        acc = jnp.zeros((CHUNK, BAND), F32)
        for bucket in range(REL_BUCKETS):
            acc = jnp.where(bkt == bucket, relb_ref[bucket, h], acc)
        bias_scr[h] = acc


def _mixer_prepare(kvc_ref, kvp_ref, pc_ref, pp_ref, kv_scr, p_scr, d_scr, block_in_seq):
    rows = kvc_ref.shape[0]
    kv_scr[:KV_HALO] = kvp_ref[...]
    kv_scr[KV_HALO:] = kvc_ref[...]
    p_scr[:POOL_HALO] = jnp.where(block_in_seq == 0, 0.0, pp_ref[...])
    p_scr[POOL_HALO:] = pc_ref[...]
    t_seq = block_in_seq * rows + lax.broadcasted_iota(jnp.int32, (rows, 1), 0)
    for g, w in enumerate(POOL_WINDOWS):
        lanes = slice(g * POOL_GROUP_DIM, (g + 1) * POOL_GROUP_DIM)
        cur = p_scr[POOL_HALO:POOL_HALO + rows, lanes]
        total = cur
        for j in range(1, w):
            total = total + p_scr[POOL_HALO - j:POOL_HALO - j + rows, lanes]
        cnt = jnp.minimum(t_seq + 1, w).astype(F32)
        d_scr[:, lanes] = (total / cnt - cur).astype(BF16)


def _mixer_quarter_steps(quarter, x_ref, q_ref, sink_ref, poolw_ref, pscale_ref, wout_ref, g2_ref,
                         bias_scr, kv_scr, d_scr, block_in_seq, x1_out, h2_out):
    rows = MIXER_QUARTER
    per_chunk = CHUNK // rows
    row0 = pl.multiple_of(quarter * rows, rows)
    chunk = quarter >> (per_chunk.bit_length() - 1)
    band = pl.ds(pl.multiple_of(chunk * CHUNK, CHUNK), BAND)
    bias_rows = pl.ds(pl.multiple_of((quarter & (per_chunk - 1)) * rows, rows), rows)
    st = {"scores": [None] * ATTN_HEADS, "probs": [None] * ATTN_HEADS, "out": [None] * ATTN_HEADS}

    def score_step(h):
        def step():
            key_slot = lax.broadcasted_iota(jnp.int32, (rows, BAND), 1) // CHUNK
            first_key_chunk = block_in_seq * (BLOCK // CHUNK) + chunk - WIN_CHUNKS
            kvh = h // ATTN_GROUP
            qh = q_ref[pl.ds(row0, rows), h * HEAD_DIM:(h + 1) * HEAD_DIM]
            kb = kv_scr[band, kvh * HEAD_DIM:(kvh + 1) * HEAD_DIM]
            s = lax.dot_general(qh, kb, (((1,), (1,)), ((), ())), preferred_element_type=F32)
            st["scores"][h] = jnp.where((key_slot + first_key_chunk) >= 0,
                                        s + bias_scr[h, bias_rows, :], NEG_INF)
        return step

    def softmax_step(h):
        def step():
            s = st["scores"][h]
            sink = sink_ref[h]
            m = jnp.maximum(jnp.max(s, axis=-1, keepdims=True), sink)
            e = jnp.exp(s - m)
            denom = jnp.sum(e, axis=-1, keepdims=True) + jnp.exp(sink - m)
            st["probs"][h] = (e / denom).astype(BF16)
        return step

    def pv_step(h):
        def step():
            kvh = h // ATTN_GROUP
            vb = kv_scr[band, D_KV + kvh * HEAD_DIM:D_KV + (kvh + 1) * HEAD_DIM]
            st["out"][h] = jnp.dot(st["probs"][h], vb, preferred_element_type=F32)
        return step

    def pool_step():
        pooled = []
        for g in range(len(POOL_WINDOWS)):
            lanes = slice(g * POOL_GROUP_DIM, (g + 1) * POOL_GROUP_DIM)
            og = jnp.dot(d_scr[pl.ds(row0, rows), lanes], poolw_ref[g], preferred_element_type=F32)
            pooled.append(og * pscale_ref[:, lanes])
        st["pooled"] = jnp.concatenate(pooled, axis=1).astype(BF16)

    def project_step():
        attn = jnp.concatenate(st["out"], axis=1).astype(BF16)
        mixed = (jnp.dot(attn, wout_ref[:D_ATTN], preferred_element_type=F32)
                 + jnp.dot(st["pooled"], wout_ref[D_ATTN:], preferred_element_type=F32))
        x1 = x_ref[pl.ds(row0, rows), :] + mixed
        x1_out[pl.ds(row0, rows), :] = x1
        h2_out[pl.ds(row0, rows), :] = _rms(x1, g2_ref[...])

    return ([score_step(h) for h in range(ATTN_HEADS)],
            [softmax_step(h) for h in range(ATTN_HEADS)],
            [pv_step(h) for h in range(ATTN_HEADS)],
            [pool_step, project_step])


def _gelu(x):
    return 0.5 * x * (1.0 + lax.erf(x * (2.0 ** -0.5)))


def _top_round_steps(state, scores, ident, values, winners):
    row = lax.broadcasted_iota(jnp.int32, (PEER_TOPK, LANES), 0)

    def make(r):
        def step():
            s, ids = state[scores], state[ident]
            m = jnp.max(s, axis=0, keepdims=True)
            idx = jnp.min(jnp.where(s == m, ids, 1e9), axis=0, keepdims=True)
            state[values] = jnp.where(row == r, m, state[values])
            state[winners] = jnp.where(row == r, idx, state[winners])
            state[scores] = jnp.where(ids == idx, -jnp.inf, s)
        return step

    return [make(r) for r in range(PEER_TOPK)]


def _candidate_tables():
    pos, pen = [], []
    for a, width in [(0, PEER_TOPK)] + [(a, SUBLANES) for a in range(1, SUBLANES)]:
        for b in range(width):
            pos.append(a * PEER_TOPK + b)
            pen.append(0.0 if (a + 1) * (b + 1) <= PEER_TOPK else -np.inf)
    for a in range(SUBLANES, PEER_TOPK):
        pos.append(a * PEER_TOPK)
        pen.append(0.0)
    return (np.asarray(pos, np.float32).reshape(-1, 1), np.asarray(pen, np.float32).reshape(-1, 1))


def _route_steps(hr_ref, wq_ref, sk_ref, pos_ref, pen_ref, h, p, val_scr, idx_scr, ids_ref, gates_ref):
    st = {}
    zeros = jnp.zeros((PEER_TOPK, LANES), F32)

    def start():
        q = jnp.dot(hr_ref[...].astype(BF16), wq_ref[h, :, p * PEER_HALF:(p + 1) * PEER_HALF],
                    preferred_element_type=F32).astype(BF16)
        st["scores"] = lax.dot_general(sk_ref[2 * h + p], q, (((1,), (1,)), ((), ())),
                                       preferred_element_type=F32)
        st["keys"] = lax.broadcasted_iota(jnp.int32, (PEER_KEYS, LANES), 0).astype(F32)
        st["val"], st["idx"] = zeros, zeros

    steps = [start] + _top_round_steps(st, "scores", "keys", "val", "idx")

    if p == 0:
        def finish():
            val_scr[...] = st["val"]
            idx_scr[...] = st["idx"]

        return steps + [finish]

    def start_head():
        v1, v2 = val_scr[...], st["val"]
        pen = jnp.broadcast_to(pen_ref[...], (pen_ref.shape[0], LANES))
        st["cand"] = jnp.concatenate(
            [v1[0:1, :] + v2]
            + [v1[a:a + 1, :] + v2[0:SUBLANES, :] for a in range(1, SUBLANES)]
            + [v1[SUBLANES:, :] + v2[0:1, :]], axis=0) + pen
        st["pos"] = jnp.broadcast_to(pos_ref[...], (pos_ref.shape[0], LANES))
        st["best"], st["win"] = zeros, zeros

    def finish_head():
        i1, i2 = idx_scr[...], st["idx"]
        pos = st["win"].astype(jnp.int32)
        pa = pos // PEER_TOPK
        pb = pos % PEER_TOPK
        e1, e2 = zeros, zeros
        for a in range(PEER_TOPK):
            e1 = e1 + jnp.where(pa == a, i1[a:a + 1, :], 0.0)
            e2 = e2 + jnp.where(pb == a, i2[a:a + 1, :], 0.0)
        best = st["best"]
        ex = jnp.exp(best - best[0:1, :])
        head_rows = pl.ds(pl.multiple_of(h * PEER_TOPK, PEER_TOPK), PEER_TOPK)
        ids_ref[head_rows, :] = (e1 * PEER_KEYS + e2).astype(jnp.int32)
        gates_ref[head_rows, :] = ex / jnp.sum(ex, axis=0, keepdims=True)

    return (steps + [start_head] + _top_round_steps(st, "cand", "pos", "best", "win")
            + [finish_head])


def _block_kernel(x_ref, q_ref, kvc_ref, kvp_ref, pc_ref, pp_ref, bkt_ref, relb_ref, sink_ref,
                  poolw_ref, pscale_ref, wout_ref, g2_ref, wq_ref, sk_ref, pos_ref, pen_ref,
                  tab_hbm, tab_slabs_hbm, out_ref,
                  bias_scr, kv_scr, p_scr, d_scr, x1_ring, h2_ring,
                  ring, ring_sems, ids_smem, ids_stage, gates_ring, val_scr, idx_scr, ids_sem,
                  *, blocks, blocks_per_seq):
    s = pl.program_id(0)
    d = x_ref.shape[1]
    chunks = d // LANES
    groups = BLOCK // EXPERT_SUB
    turns = groups // EXPERT_RING
    row_tiles = tab_hbm.shape[1]
    pitch = row_tiles + 1
    half = PEER_SLOTS // 2
    gate_slots = gates_ring.shape[0]

    @pl.when(s == 0)
    def _bias():
        _build_bias(bkt_ref, relb_ref, bias_scr)

    mixed_slot = s % MIX_RING
    block_in_seq = jnp.minimum(s, blocks - 1) % blocks_per_seq
    _mixer_prepare(kvc_ref, kvp_ref, pc_ref, pp_ref, kv_scr, p_scr, d_scr, block_in_seq)

    def mixer_steps(quarter):
        return _mixer_quarter_steps(quarter, x_ref, q_ref, sink_ref, poolw_ref, pscale_ref, wout_ref,
                                    g2_ref, bias_scr, kv_scr, d_scr, block_in_seq,
                                    x1_ring.at[mixed_slot], h2_ring.at[mixed_slot])

    @pl.when(s < MIX_AHEAD)
    def _mix_only():
        def quarter(qr, carry):
            for steps in mixer_steps(qr):
                for step in steps:
                    step()
            return carry

        lax.fori_loop(0, turns, quarter, 0)

    routed_h2 = h2_ring.at[(s + MIX_RING - 1) % MIX_RING]
    routed_gates = gates_ring.at[(s + gate_slots - 1) % gate_slots]

    def route_steps(h, p, stage):
        return _route_steps(routed_h2, wq_ref, sk_ref, pos_ref, pen_ref, h, p, val_scr, idx_scr,
                            ids_stage.at[stage], routed_gates)

    def ids_copy(stage, parity):
        col0 = parity * BLOCK
        if not isinstance(col0, int):
            col0 = pl.multiple_of(col0, BLOCK)
        return pltpu.make_async_copy(ids_stage.at[stage], ids_smem.at[:, pl.ds(col0, BLOCK)],
                                     ids_sem.at[0])

    @pl.when((s >= 1) & (s < MIX_AHEAD))
    def _route_only():
        def head(h, carry):
            for p in range(2):
                for step in route_steps(h, p, 0):
                    step()
            return carry

        lax.fori_loop(0, PEER_HEADS, head, 0)
        copy = ids_copy(0, (s + 1) & 1)
        copy.start()
        copy.wait()

    @pl.when(s >= MIX_AHEAD)
    def _experts():
        i = s - MIX_AHEAD
        last = blocks - 1
        h2_blk = h2_ring.at[i % MIX_RING]
        x1_blk = x1_ring.at[i % MIX_RING]

        def issue(group, slot, tt, k0, k1):
            if isinstance(group, int):
                block_off, in_block = divmod(group, groups)
            else:
                block_off, in_block = group >> (groups.bit_length() - 1), group & (groups - 1)
            col = ((i + block_off) & 1) * BLOCK + in_block * EXPERT_SUB + tt
            for k in range(k0, k1):
                row = ids_smem[k, col]
                dst = ring.at[slot, pl.ds((tt * PEER_SLOTS + k) * pitch, row_tiles)]
                pltpu.make_async_copy(tab_hbm.at[row], dst, ring_sems.at[slot]).start(priority=k % 2)

        def wait(slot):
            rows_moved = EXPERT_SUB * PEER_SLOTS * row_tiles
            pltpu.make_async_copy(tab_slabs_hbm.at[pl.ds(0, rows_moved)],
                                  ring.at[slot, pl.ds(0, rows_moved)], ring_sems.at[slot]).wait()

        @pl.when(i == 0)
        def _first_rows():
            for a in range(EXPERT_AHEAD):
                for tt in range(EXPERT_SUB):
                    issue(a, a, tt, 0, PEER_SLOTS)

        lane = lax.broadcasted_iota(jnp.int32, (PEER_SLOTS, LANES), 1)
        sub = lax.broadcasted_iota(jnp.int32, (EXPERT_SUB, LANES), 0)

        def group_body(g, slot, side_steps):
            ahead = (slot + EXPERT_AHEAD) % EXPERT_RING
            per_issue = -(-len(side_steps) // (2 * EXPERT_SUB))
            pending = list(side_steps)

            def run_side(n):
                for _ in range(min(n, len(pending))):
                    pending.pop(0)()

            wait(slot)
            tok_rows = pl.ds(pl.multiple_of(g * EXPERT_SUB, EXPERT_SUB), EXPERT_SUB)
            hsub = h2_blk[tok_rows, :]

            def tile(tt, c):
                return ring[slot, pl.ds(tt * PEER_SLOTS * pitch + c, PEER_SLOTS, stride=pitch), :]

            def cols(j):
                return slice(j * LANES, (j + 1) * LANES)

            acts = jnp.zeros((PEER_SLOTS, LANES), F32)
            per_tile = half // chunks
            for tt in range(EXPERT_SUB):
                acc = None
                for c in range(chunks):
                    issue(g + EXPERT_AHEAD, ahead, tt, c * per_tile, (c + 1) * per_tile)
                    if c == 0:
                        run_side(per_issue)
                    term = tile(tt, c) * hsub[tt:tt + 1, cols(c)]
                    acc = term if acc is None else acc + term
                acts = jnp.where(lane == tt, jnp.sum(acc, axis=1, keepdims=True), acts)
            gates = pltpu.roll(gates_ring[i % gate_slots], (BLOCK - g * EXPERT_SUB) % BLOCK, 1)
            coef = gates * _gelu(acts)
            o = [jnp.zeros((EXPERT_SUB, LANES), F32) for _ in range(chunks)]
            for tt in range(EXPERT_SUB):
                cc = coef[:, tt:tt + 1]
                for c in range(chunks):
                    issue(g + EXPERT_AHEAD, ahead, tt, half + c * per_tile, half + (c + 1) * per_tile)
                    if c == 0:
                        run_side(per_issue)
                    oc = jnp.sum(tile(tt, chunks + c) * cc, axis=0, keepdims=True)
                    o[c] = jnp.where(sub == tt, oc, o[c])
            run_side(len(pending))
            for c in range(chunks):
                out_ref[tok_rows, cols(c)] = x1_blk[tok_rows, cols(c)] + o[c]

        def ring_turn(q, carry):
            @pl.when((q == turns - 1) & (i > 0))
            def _ids_landed():
                ids_copy(0, 0).wait()

            mixing = mixer_steps(q)
            for r in range(EXPERT_RING):
                side = route_steps(q * (EXPERT_RING // 2) + r // 2, r % 2, i & 1) + mixing[r]
                group_body(q * EXPERT_RING + r, r, side)
            return carry

        lax.fori_loop(0, turns, ring_turn, 0)
        ids_copy(i & 1, i & 1).start()

        @pl.when(i == last)
        def _drain():
            ids_copy(0, 0).wait()
            for a in range(EXPERT_AHEAD):
                wait((groups + a) % EXPERT_RING)


def _block_pipeline(x2, q, kv, p, rel_bias, sinks, pool_w, pool_scale, w_out, g2,
                    peer_wq, peer_subkeys, table, seq):
    T, D = x2.shape
    blocks = T // BLOCK
    groups = BLOCK // EXPERT_SUB
    sub_rows = EXPERT_SUB * PEER_SLOTS
    n_experts, row_tiles, _ = table.shape
    assert row_tiles == 2 * D // LANES
    assert groups % EXPERT_RING == 0 and groups == 2 * PEER_HEADS and EXPERT_RING % 2 == 0
    assert groups & (groups - 1) == 0 and BLOCK == LANES and BLOCK == KV_HALO
    assert ROUTE_AHEAD == 2 and blocks >= 1 and n_experts % SUBLANES == 0
    assert seq % BLOCK == 0 and BLOCK % CHUNK == 0 and BLOCK % POOL_HALO == 0
    wq = peer_wq.astype(BF16).reshape(D, PEER_HEADS, 2 * PEER_HALF).transpose(1, 0, 2)
    sk = peer_subkeys.astype(BF16).reshape(2 * PEER_HEADS, PEER_KEYS, PEER_HALF)
    pos, pen = _candidate_tables()
    n_cand = pos.shape[0]
    table_slabs = table.reshape(n_experts * row_tiles, LANES)

    mixed = lambda s: jnp.minimum(s, blocks - 1)
    cur = lambda c: pl.BlockSpec((BLOCK, c), lambda s: (mixed(s), 0))
    halo = lambda rows, c: pl.BlockSpec(
        (rows, c), lambda s: (jnp.maximum(mixed(s) * (BLOCK // rows) - 1, 0), 0))
    const = lambda *shape: pl.BlockSpec(shape, lambda s: (0,) * len(shape))
    smem = pl.BlockSpec(memory_space=pltpu.SMEM)
    hbm = pl.BlockSpec(memory_space=pl.ANY)
    return pl.pallas_call(
        functools.partial(_block_kernel, blocks=blocks, blocks_per_seq=seq // BLOCK),
        grid=(blocks + MIX_AHEAD,),
        in_specs=[cur(D), cur(D_ATTN), cur(2 * D_KV), halo(KV_HALO, 2 * D_KV),
                  cur(D_POOL), halo(POOL_HALO, D_POOL),
                  const(CHUNK, BAND), smem, smem,
                  const(len(POOL_WINDOWS), POOL_GROUP_DIM, POOL_GROUP_DIM),
                  const(1, D_POOL), const(D, D), const(1, D),
                  const(PEER_HEADS, D, 2 * PEER_HALF), const(2 * PEER_HEADS, PEER_KEYS, PEER_HALF),
                  const(n_cand, 1), const(n_cand, 1),
                  hbm, hbm],
        out_specs=pl.BlockSpec((BLOCK, D), lambda s: (jnp.maximum(s - MIX_AHEAD, 0), 0)),
        out_shape=jax.ShapeDtypeStruct((T, D), F32),
        scratch_shapes=[pltpu.VMEM((ATTN_HEADS, CHUNK, BAND), F32),
                        pltpu.VMEM((KV_HALO + BLOCK, 2 * D_KV), BF16),
                        pltpu.VMEM((POOL_HALO + BLOCK, D_POOL), F32),
                        pltpu.VMEM((BLOCK, D_POOL), BF16),
                        pltpu.VMEM((MIX_RING, BLOCK, D), F32),
                        pltpu.VMEM((MIX_RING, BLOCK, D), F32),
                        pltpu.VMEM((EXPERT_RING, sub_rows * (row_tiles + 1), LANES), F32),
                        pltpu.SemaphoreType.DMA((EXPERT_RING,)),
                        pltpu.SMEM((PEER_SLOTS, 2 * BLOCK), jnp.int32),
                        pltpu.VMEM((2, PEER_SLOTS, BLOCK), jnp.int32),
                        pltpu.VMEM((ROUTE_AHEAD + 1, PEER_SLOTS, BLOCK), F32),
                        pltpu.VMEM((PEER_TOPK, BLOCK), F32),
                        pltpu.VMEM((PEER_TOPK, BLOCK), F32),
                        pltpu.SemaphoreType.DMA((1,))],
        compiler_params=pltpu.CompilerParams(dimension_semantics=("arbitrary",),
                                             vmem_limit_bytes=VMEM_LIMIT),
        name="block_pipeline",
    )(x2, q, kv, kv, p, p, jnp.asarray(_t5_bucket_table()), rel_bias.astype(F32),
      sinks.astype(F32), pool_w.astype(BF16), pool_scale.reshape(1, D_POOL),
      w_out.astype(BF16), g2.reshape(1, D), wq, sk, jnp.asarray(pos), jnp.asarray(pen),
      table, table_slabs)


def kernel(x, norm1_g, w_in, q_norm_g, k_norm_g, attn_sinks, rel_bias, pool_w, pool_scale, w_out,
           norm2_g, peer_wq, peer_subkeys, peer_u, peer_v):
    batch, seq, d_model = x.shape
    depth = norm1_g.shape[0]
    assert (batch * seq) % IN_PROJ_ROWS == 0
    x2 = x.reshape(batch * seq, d_model)
    for l in range(depth):
        q, kv, p, table = _in_proj(x2, norm1_g[l], w_in[l], q_norm_g[l], k_norm_g[l],
                                   peer_u[l], peer_v[l])
        x2 = _block_pipeline(x2, q, kv, p, rel_bias, attn_sinks[l], pool_w[l], pool_scale[l],
                             w_out[l], norm2_g[l], peer_wq[l], peer_subkeys[l], table, seq)
    return x2.reshape(batch, seq, d_model)
```

```python
import functools
import math

import jax
import jax.numpy as jnp
import numpy as np
from jax import lax
from jax.experimental import pallas as pl
from jax.experimental.pallas import tpu as pltpu

F32 = jnp.float32
BF16 = jnp.bfloat16

CHUNK = 64
ATTN_HEADS = 8
ATTN_KV_HEADS = 2
HEAD_DIM = 64
ATTN_GROUP = ATTN_HEADS // ATTN_KV_HEADS
WIN_CHUNKS = 2
BAND = (WIN_CHUNKS + 1) * CHUNK
D_ATTN = ATTN_HEADS * HEAD_DIM
D_KV = ATTN_KV_HEADS * HEAD_DIM
POOL_WINDOWS = (2, 4, 8, 16)
POOL_GROUP_DIM = 128
D_POOL = len(POOL_WINDOWS) * POOL_GROUP_DIM
REL_BUCKETS = 32
REL_MAX_DIST = 128
PEER_KEYS = 128
PEER_HEADS = 8
PEER_TOPK = 16
PEER_HALF = 128
PEER_SLOTS = PEER_HEADS * PEER_TOPK
EPS = 1e-6
NEG_INF = -1e30

LANES = 128
SUBLANES = 8

IN_PROJ_ROWS = 512
POOL_HALO = 16
KV_HALO = WIN_CHUNKS * CHUNK
BLOCK = 128
EXPERT_SUB = 8
EXPERT_AHEAD = 3
EXPERT_RING = EXPERT_AHEAD + 1
ROUTE_AHEAD = 2
MIX_AHEAD = ROUTE_AHEAD + 1
MIX_RING = MIX_AHEAD + 1
MIXER_QUARTER = BLOCK // (BLOCK // EXPERT_SUB // EXPERT_RING)
VMEM_LIMIT = 58 * 1024 * 1024


def _rms(x, g):
    return x * lax.rsqrt(jnp.mean(x * x, axis=-1, keepdims=True) + EPS) * g


def _segment_sumsq(x, ind):
    sq = x * x
    hi = sq.astype(BF16)
    lo = (sq - hi.astype(F32)).astype(BF16)
    return (jnp.dot(hi, ind, preferred_element_type=F32)
            + jnp.dot(lo, ind, preferred_element_type=F32))


def _in_proj_kernel(x_ref, g1_ref, w_ref, gq_ref, gk_ref, indq_ref, indk_ref, u_ref, v_ref,
                    q_ref, kv_ref, p_ref, tab_ref):
    experts = u_ref.shape[0]
    tiles = u_ref.shape[1] // LANES
    for t, src in enumerate((u_ref, v_ref)):
        for c in range(tiles):
            tab_ref[pl.ds(t * tiles + c, experts, stride=2 * tiles), :] = src[:, c * LANES:(c + 1) * LANES]

    h = _rms(x_ref[...], g1_ref[...])
    y = jnp.dot(h.astype(BF16), w_ref[...], preferred_element_type=F32)
    q = y[:, :D_ATTN]
    k = y[:, D_ATTN:D_ATTN + D_KV]
    v = y[:, D_ATTN + D_KV:D_ATTN + 2 * D_KV]
    qn = q * lax.rsqrt(_segment_sumsq(q, indq_ref[...]) * (1.0 / HEAD_DIM) + EPS) * gq_ref[...]
    kn = k * lax.rsqrt(_segment_sumsq(k, indk_ref[...]) * (1.0 / HEAD_DIM) + EPS) * gk_ref[...]
    q_ref[...] = (qn * (HEAD_DIM ** -0.5)).astype(BF16)
    kv_ref[:, :D_KV] = kn.astype(BF16)
    kv_ref[:, D_KV:] = v.astype(BF16)
    p_ref[...] = y[:, D_ATTN + 2 * D_KV:]


def _in_proj(x2, g1, w_in, gq, gk, peer_u, peer_v):
    T, D = x2.shape
    d_in = w_in.shape[1]
    steps = T // IN_PROJ_ROWS
    n_experts, d_exp = peer_u.shape
    per_step = n_experts // steps
    assert n_experts % steps == 0 and per_step % SUBLANES == 0
    row_tiles = 2 * d_exp // LANES
    head_of = np.arange(D_ATTN) // HEAD_DIM
    indq = jnp.asarray(head_of[:, None] == head_of[None, :], BF16)
    indk = indq[:D_KV, :D_KV]
    full = lambda r, c: pl.BlockSpec((r, c), lambda i: (0, 0))
    rows = lambda c: pl.BlockSpec((IN_PROJ_ROWS, c), lambda i: (i, 0))
    exps = lambda r, c: pl.BlockSpec((r, c), lambda i: (i, 0))
    q, kv, p, table = pl.pallas_call(
        _in_proj_kernel,
        grid=(steps,),
        in_specs=[rows(D), full(1, D), full(D, d_in), full(1, D_ATTN), full(1, D_KV),
                  full(D_ATTN, D_ATTN), full(D_KV, D_KV),
                  exps(per_step, d_exp), exps(per_step, d_exp)],
        out_specs=[rows(D_ATTN), rows(2 * D_KV), rows(D_POOL), exps(per_step * row_tiles, LANES)],
        out_shape=[jax.ShapeDtypeStruct((T, D_ATTN), BF16),
                   jax.ShapeDtypeStruct((T, 2 * D_KV), BF16),
                   jax.ShapeDtypeStruct((T, D_POOL), F32),
                   jax.ShapeDtypeStruct((n_experts * row_tiles, LANES), F32)],
        compiler_params=pltpu.CompilerParams(dimension_semantics=("arbitrary",),
                                             vmem_limit_bytes=VMEM_LIMIT),
        name="in_proj",
    )(x2, g1.reshape(1, D), w_in.astype(BF16),
      jnp.tile(gq, ATTN_HEADS).reshape(1, D_ATTN), jnp.tile(gk, ATTN_KV_HEADS).reshape(1, D_KV),
      indq, indk, peer_u, peer_v)
    return q, kv, p, table.reshape(n_experts, row_tiles, LANES)


def _t5_bucket_table():
    i = np.arange(CHUNK)[:, None]
    j = np.arange(BAND)[None, :]
    rel = (j - WIN_CHUNKS * CHUNK) - i
    nb = REL_BUCKETS // 2
    max_exact = nb // 2
    base = np.where(rel > 0, nb, 0)
    n = np.abs(rel)
    nf = np.maximum(n, 1).astype(np.float64)
    large = max_exact + (np.log(nf / max_exact) / math.log(REL_MAX_DIST / max_exact)
                         * (nb - max_exact)).astype(np.int32)
    large = np.minimum(large, nb - 1)
    return (base + np.where(n < max_exact, n, large)).astype(np.int32)


def _build_bias(bkt_ref, relb_ref, bias_scr):
    bkt = bkt_ref[...]
    for h in range(ATTN_HEADS):
        acc = jnp.zeros((CHUNK, BAND), F32)
        for bucket in range(REL_BUCKETS):
            acc = jnp.where(bkt == bucket, relb_ref[bucket, h], acc)
        bias_scr[h] = acc


def _mixer_prepare(kvc_ref, kvp_ref, pc_ref, pp_ref, kv_scr, p_scr, d_scr, block_in_seq):
    rows = kvc_ref.shape[0]
    kv_scr[:KV_HALO] = kvp_ref[...]
    kv_scr[KV_HALO:] = kvc_ref[...]
    p_scr[:POOL_HALO] = jnp.where(block_in_seq == 0, 0.0, pp_ref[...])
    p_scr[POOL_HALO:] = pc_ref[...]
    t_seq = block_in_seq * rows + lax.broadcasted_iota(jnp.int32, (rows, 1), 0)
    for g, w in enumerate(POOL_WINDOWS):
        lanes = slice(g * POOL_GROUP_DIM, (g + 1) * POOL_GROUP_DIM)
        cur = p_scr[POOL_HALO:POOL_HALO + rows, lanes]
        total = cur
        for j in range(1, w):
            total = total + p_scr[POOL_HALO - j:POOL_HALO - j + rows, lanes]
        cnt = jnp.minimum(t_seq + 1, w).astype(F32)
        d_scr[:, lanes] = (total / cnt - cur).astype(BF16)


def _mixer_quarter_steps(quarter, x_ref, q_ref, sink_ref, poolw_ref, pscale_ref, wout_ref, g2_ref,
                         bias_scr, kv_scr, d_scr, block_in_seq, x1_out, h2_out):
    rows = MIXER_QUARTER
    per_chunk = CHUNK // rows
    row0 = pl.multiple_of(quarter * rows, rows)
    chunk = quarter >> (per_chunk.bit_length() - 1)
    band = pl.ds(pl.multiple_of(chunk * CHUNK, CHUNK), BAND)
    bias_rows = pl.ds(pl.multiple_of((quarter & (per_chunk - 1)) * rows, rows), rows)
    st = {"scores": [None] * ATTN_HEADS, "probs": [None] * ATTN_HEADS, "out": [None] * ATTN_HEADS}

    def score_step(h):
        def step():
            key_slot = lax.broadcasted_iota(jnp.int32, (rows, BAND), 1) // CHUNK
            first_key_chunk = block_in_seq * (BLOCK // CHUNK) + chunk - WIN_CHUNKS
            kvh = h // ATTN_GROUP
            qh = q_ref[pl.ds(row0, rows), h * HEAD_DIM:(h + 1) * HEAD_DIM]
            kb = kv_scr[band, kvh * HEAD_DIM:(kvh + 1) * HEAD_DIM]
            s = lax.dot_general(qh, kb, (((1,), (1,)), ((), ())), preferred_element_type=F32)
            st["scores"][h] = jnp.where((key_slot + first_key_chunk) >= 0,
                                        s + bias_scr[h, bias_rows, :], NEG_INF)
        return step

    def softmax_step(h):
        def step():
            s = st["scores"][h]
            sink = sink_ref[h]
            m = jnp.maximum(jnp.max(s, axis=-1, keepdims=True), sink)
            e = jnp.exp(s - m)
            denom = jnp.sum(e, axis=-1, keepdims=True) + jnp.exp(sink - m)
            st["probs"][h] = (e / denom).astype(BF16)
        return step

    def pv_step(h):
        def step():
            kvh = h // ATTN_GROUP
            vb = kv_scr[band, D_KV + kvh * HEAD_DIM:D_KV + (kvh + 1) * HEAD_DIM]
            st["out"][h] = jnp.dot(st["probs"][h], vb, preferred_element_type=F32)
        return step

    def pool_step():
        pooled = []
        for g in range(len(POOL_WINDOWS)):
            lanes = slice(g * POOL_GROUP_DIM, (g + 1) * POOL_GROUP_DIM)
            og = jnp.dot(d_scr[pl.ds(row0, rows), lanes], poolw_ref[g], preferred_element_type=F32)
            pooled.append(og * pscale_ref[:, lanes])
        st["pooled"] = jnp.concatenate(pooled, axis=1).astype(BF16)

    def project_step():
        attn = jnp.concatenate(st["out"], axis=1).astype(BF16)
        mixed = (jnp.dot(attn, wout_ref[:D_ATTN], preferred_element_type=F32)
                 + jnp.dot(st["pooled"], wout_ref[D_ATTN:], preferred_element_type=F32))
        x1 = x_ref[pl.ds(row0, rows), :] + mixed
        x1_out[pl.ds(row0, rows), :] = x1
        h2_out[pl.ds(row0, rows), :] = _rms(x1, g2_ref[...])

    return ([score_step(h) for h in range(ATTN_HEADS)],
            [softmax_step(h) for h in range(ATTN_HEADS)],
            [pv_step(h) for h in range(ATTN_HEADS)],
            [pool_step, project_step])


def _gelu(x):
    return 0.5 * x * (1.0 + lax.erf(x * (2.0 ** -0.5)))


def _top_round_steps(state, scores, ident, values, winners):
    row = lax.broadcasted_iota(jnp.int32, (PEER_TOPK, LANES), 0)

    def make(r):
        def step():
            s, ids = state[scores], state[ident]
            m = jnp.max(s, axis=0, keepdims=True)
            idx = jnp.min(jnp.where(s == m, ids, 1e9), axis=0, keepdims=True)
            state[values] = jnp.where(row == r, m, state[values])
            state[winners] = jnp.where(row == r, idx, state[winners])
            state[scores] = jnp.where(ids == idx, -jnp.inf, s)
        return step

    return [make(r) for r in range(PEER_TOPK)]


def _candidate_tables():
    pos, pen = [], []
    for a, width in [(0, PEER_TOPK)] + [(a, SUBLANES) for a in range(1, SUBLANES)]:
        for b in range(width):
            pos.append(a * PEER_TOPK + b)
            pen.append(0.0 if (a + 1) * (b + 1) <= PEER_TOPK else -np.inf)
    for a in range(SUBLANES, PEER_TOPK):
        pos.append(a * PEER_TOPK)
        pen.append(0.0)
    return (np.asarray(pos, np.float32).reshape(-1, 1), np.asarray(pen, np.float32).reshape(-1, 1))


def _route_steps(hr_ref, wq_ref, sk_ref, pos_ref, pen_ref, h, p, val_scr, idx_scr, ids_ref, gates_ref):
    st = {}
    zeros = jnp.zeros((PEER_TOPK, LANES), F32)

    def start():
        q = jnp.dot(hr_ref[...].astype(BF16), wq_ref[h, :, p * PEER_HALF:(p + 1) * PEER_HALF],
                    preferred_element_type=F32).astype(BF16)
        st["scores"] = lax.dot_general(sk_ref[2 * h + p], q, (((1,), (1,)), ((), ())),
                                       preferred_element_type=F32)
        st["keys"] = lax.broadcasted_iota(jnp.int32, (PEER_KEYS, LANES), 0).astype(F32)
        st["val"], st["idx"] = zeros, zeros

    steps = [start] + _top_round_steps(st, "scores", "keys", "val", "idx")

    if p == 0:
        def finish():
            val_scr[...] = st["val"]
            idx_scr[...] = st["idx"]

        return steps + [finish]

    def start_head():
        v1, v2 = val_scr[...], st["val"]
        pen = jnp.broadcast_to(pen_ref[...], (pen_ref.shape[0], LANES))
        st["cand"] = jnp.concatenate(
            [v1[0:1, :] + v2]
            + [v1[a:a + 1, :] + v2[0:SUBLANES, :] for a in range(1, SUBLANES)]
            + [v1[SUBLANES:, :] + v2[0:1, :]], axis=0) + pen
        st["pos"] = jnp.broadcast_to(pos_ref[...], (pos_ref.shape[0], LANES))
        st["best"], st["win"] = zeros, zeros

    def finish_head():
        i1, i2 = idx_scr[...], st["idx"]
        pos = st["win"].astype(jnp.int32)
        pa = pos // PEER_TOPK
        pb = pos % PEER_TOPK
        e1, e2 = zeros, zeros
        for a in range(PEER_TOPK):
            e1 = e1 + jnp.where(pa == a, i1[a:a + 1, :], 0.0)
            e2 = e2 + jnp.where(pb == a, i2[a:a + 1, :], 0.0)
        best = st["best"]
        ex = jnp.exp(best - best[0:1, :])
        head_rows = pl.ds(pl.multiple_of(h * PEER_TOPK, PEER_TOPK), PEER_TOPK)
        ids_ref[head_rows, :] = (e1 * PEER_KEYS + e2).astype(jnp.int32)
        gates_ref[head_rows, :] = ex / jnp.sum(ex, axis=0, keepdims=True)

    return (steps + [start_head] + _top_round_steps(st, "cand", "pos", "best", "win")
            + [finish_head])


def _block_kernel(x_ref, q_ref, kvc_ref, kvp_ref, pc_ref, pp_ref, bkt_ref, relb_ref, sink_ref,
                  poolw_ref, pscale_ref, wout_ref, g2_ref, wq_ref, sk_ref, pos_ref, pen_ref,
                  tab_hbm, tab_slabs_hbm, out_ref,
                  bias_scr, kv_scr, p_scr, d_scr, x1_ring, h2_ring,
                  ring, ring_sems, ids_smem, ids_stage, gates_ring, val_scr, idx_scr, ids_sem,
                  *, blocks, blocks_per_seq):
    s = pl.program_id(0)
    d = x_ref.shape[1]
    chunks = d // LANES
    groups = BLOCK // EXPERT_SUB
    turns = groups // EXPERT_RING
    row_tiles = tab_hbm.shape[1]
    pitch = row_tiles + 1
    half = PEER_SLOTS // 2
    gate_slots = gates_ring.shape[0]

    @pl.when(s == 0)
    def _bias():
        _build_bias(bkt_ref, relb_ref, bias_scr)

    mixed_slot = s % MIX_RING
    block_in_seq = jnp.minimum(s, blocks - 1) % blocks_per_seq
    _mixer_prepare(kvc_ref, kvp_ref, pc_ref, pp_ref, kv_scr, p_scr, d_scr, block_in_seq)

    def mixer_steps(quarter):
        return _mixer_quarter_steps(quarter, x_ref, q_ref, sink_ref, poolw_ref, pscale_ref, wout_ref,
                                    g2_ref, bias_scr, kv_scr, d_scr, block_in_seq,
                                    x1_ring.at[mixed_slot], h2_ring.at[mixed_slot])

    @pl.when(s < MIX_AHEAD)
    def _mix_only():
        def quarter(qr, carry):
            for steps in mixer_steps(qr):
                for step in steps:
                    step()
            return carry

        lax.fori_loop(0, turns, quarter, 0)

    routed_h2 = h2_ring.at[(s + MIX_RING - 1) % MIX_RING]
    routed_gates = gates_ring.at[(s + gate_slots - 1) % gate_slots]

    def route_steps(h, p, stage):
        return _route_steps(routed_h2, wq_ref, sk_ref, pos_ref, pen_ref, h, p, val_scr, idx_scr,
                            ids_stage.at[stage], routed_gates)

    def ids_copy(stage, parity):
        col0 = parity * BLOCK
        if not isinstance(col0, int):
            col0 = pl.multiple_of(col0, BLOCK)
        return pltpu.make_async_copy(ids_stage.at[stage], ids_smem.at[:, pl.ds(col0, BLOCK)],
                                     ids_sem.at[0])

    @pl.when((s >= 1) & (s < MIX_AHEAD))
    def _route_only():
        def head(h, carry):
            for p in range(2):
                for step in route_steps(h, p, 0):
                    step()
            return carry

        lax.fori_loop(0, PEER_HEADS, head, 0)
        copy = ids_copy(0, (s + 1) & 1)
        copy.start()
        copy.wait()

    @pl.when(s >= MIX_AHEAD)
    def _experts():
        i = s - MIX_AHEAD
        last = blocks - 1
        h2_blk = h2_ring.at[i % MIX_RING]
        x1_blk = x1_ring.at[i % MIX_RING]

        def issue(group, slot, tt, k0, k1):
            if isinstance(group, int):
                block_off, in_block = divmod(group, groups)
            else:
                block_off, in_block = group >> (groups.bit_length() - 1), group & (groups - 1)
            col = ((i + block_off) & 1) * BLOCK + in_block * EXPERT_SUB + tt
            for k in range(k0, k1):
                row = ids_smem[k, col]
                dst = ring.at[slot, pl.ds((tt * PEER_SLOTS + k) * pitch, row_tiles)]
                pltpu.make_async_copy(tab_hbm.at[row], dst, ring_sems.at[slot]).start(priority=k % 2)

        def wait(slot):
            rows_moved = EXPERT_SUB * PEER_SLOTS * row_tiles
            pltpu.make_async_copy(tab_slabs_hbm.at[pl.ds(0, rows_moved)],
                                  ring.at[slot, pl.ds(0, rows_moved)], ring_sems.at[slot]).wait()

        @pl.when(i == 0)
        def _first_rows():
            for a in range(EXPERT_AHEAD):
                for tt in range(EXPERT_SUB):
                    issue(a, a, tt, 0, PEER_SLOTS)

        lane = lax.broadcasted_iota(jnp.int32, (PEER_SLOTS, LANES), 1)
        sub = lax.broadcasted_iota(jnp.int32, (EXPERT_SUB, LANES), 0)

        def group_body(g, slot, side_steps):
            ahead = (slot + EXPERT_AHEAD) % EXPERT_RING
            per_issue = -(-len(side_steps) // (2 * EXPERT_SUB))
            pending = list(side_steps)

            def run_side(n):
                for _ in range(min(n, len(pending))):
                    pending.pop(0)()

            wait(slot)
            tok_rows = pl.ds(pl.multiple_of(g * EXPERT_SUB, EXPERT_SUB), EXPERT_SUB)
            hsub = h2_blk[tok_rows, :]

            parts = 2
            part_slots = PEER_SLOTS // parts

            def tile(tt, c, part):
                first = (tt * PEER_SLOTS + part * part_slots) * pitch + c
                return ring[slot, pl.ds(first, part_slots, stride=pitch), :]

            def cols(j):
                return slice(j * LANES, (j + 1) * LANES)

            acts = jnp.zeros((PEER_SLOTS, LANES), F32)
            per_slice = half // (chunks * parts)
            for tt in range(EXPERT_SUB):
                acc = [None] * parts
                for c in range(chunks):
                    for part in range(parts):
                        n = c * parts + part
                        issue(g + EXPERT_AHEAD, ahead, tt, n * per_slice, (n + 1) * per_slice)
                        if n == 0:
                            run_side(per_issue)
                        term = tile(tt, c, part) * hsub[tt:tt + 1, cols(c)]
                        acc[part] = term if acc[part] is None else acc[part] + term
                sums = jnp.concatenate([jnp.sum(a, axis=1, keepdims=True) for a in acc], axis=0)
                acts = jnp.where(lane == tt, sums, acts)
            gates = pltpu.roll(gates_ring[i % gate_slots], (BLOCK - g * EXPERT_SUB) % BLOCK, 1)
            coef = gates * _gelu(acts)
            o = [jnp.zeros((EXPERT_SUB, LANES), F32) for _ in range(chunks)]
            for tt in range(EXPERT_SUB):
                cc = [coef[part * part_slots:(part + 1) * part_slots, tt:tt + 1] for part in range(parts)]
                for c in range(chunks):
                    oc = None
                    for part in range(parts):
                        n = c * parts + part
                        issue(g + EXPERT_AHEAD, ahead, tt, half + n * per_slice, half + (n + 1) * per_slice)
                        if n == 0:
                            run_side(per_issue)
                        po = jnp.sum(tile(tt, chunks + c, part) * cc[part], axis=0, keepdims=True)
                        oc = po if oc is None else oc + po
                    o[c] = jnp.where(sub == tt, oc, o[c])
            run_side(len(pending))
            for c in range(chunks):
                out_ref[tok_rows, cols(c)] = x1_blk[tok_rows, cols(c)] + o[c]

        def ring_turn(q, carry):
            @pl.when((q == turns - 1) & (i > 0))
            def _ids_landed():
                ids_copy(0, 0).wait()

            mixing = mixer_steps(q)
            for r in range(EXPERT_RING):
                side = route_steps(q * (EXPERT_RING // 2) + r // 2, r % 2, i & 1) + mixing[r]
                group_body(q * EXPERT_RING + r, r, side)
            return carry

        lax.fori_loop(0, turns, ring_turn, 0)
        ids_copy(i & 1, i & 1).start()

        @pl.when(i == last)
        def _drain():
            ids_copy(0, 0).wait()
            for a in range(EXPERT_AHEAD):
                wait((groups + a) % EXPERT_RING)


def _block_pipeline(x2, q, kv, p, rel_bias, sinks, pool_w, pool_scale, w_out, g2,
                    peer_wq, peer_subkeys, table, seq):
    T, D = x2.shape
    blocks = T // BLOCK
    groups = BLOCK // EXPERT_SUB
    sub_rows = EXPERT_SUB * PEER_SLOTS
    n_experts, row_tiles, _ = table.shape
    assert row_tiles == 2 * D // LANES
    assert groups % EXPERT_RING == 0 and groups == 2 * PEER_HEADS and EXPERT_RING % 2 == 0
    assert groups & (groups - 1) == 0 and BLOCK == LANES and BLOCK == KV_HALO
    assert ROUTE_AHEAD == 2 and blocks >= 1 and n_experts % SUBLANES == 0
    assert seq % BLOCK == 0 and BLOCK % CHUNK == 0 and BLOCK % POOL_HALO == 0
    wq = peer_wq.astype(BF16).reshape(D, PEER_HEADS, 2 * PEER_HALF).transpose(1, 0, 2)
    sk = peer_subkeys.astype(BF16).reshape(2 * PEER_HEADS, PEER_KEYS, PEER_HALF)
    pos, pen = _candidate_tables()
    n_cand = pos.shape[0]
    table_slabs = table.reshape(n_experts * row_tiles, LANES)

    mixed = lambda s: jnp.minimum(s, blocks - 1)
    cur = lambda c: pl.BlockSpec((BLOCK, c), lambda s: (mixed(s), 0))
    halo = lambda rows, c: pl.BlockSpec(
        (rows, c), lambda s: (jnp.maximum(mixed(s) * (BLOCK // rows) - 1, 0), 0))
    const = lambda *shape: pl.BlockSpec(shape, lambda s: (0,) * len(shape))
    smem = pl.BlockSpec(memory_space=pltpu.SMEM)
    hbm = pl.BlockSpec(memory_space=pl.ANY)
    return pl.pallas_call(
        functools.partial(_block_kernel, blocks=blocks, blocks_per_seq=seq // BLOCK),
        grid=(blocks + MIX_AHEAD,),
        in_specs=[cur(D), cur(D_ATTN), cur(2 * D_KV), halo(KV_HALO, 2 * D_KV),
                  cur(D_POOL), halo(POOL_HALO, D_POOL),
                  const(CHUNK, BAND), smem, smem,
                  const(len(POOL_WINDOWS), POOL_GROUP_DIM, POOL_GROUP_DIM),
                  const(1, D_POOL), const(D, D), const(1, D),
                  const(PEER_HEADS, D, 2 * PEER_HALF), const(2 * PEER_HEADS, PEER_KEYS, PEER_HALF),
                  const(n_cand, 1), const(n_cand, 1),
                  hbm, hbm],
        out_specs=pl.BlockSpec((BLOCK, D), lambda s: (jnp.maximum(s - MIX_AHEAD, 0), 0)),
        out_shape=jax.ShapeDtypeStruct((T, D), F32),
        scratch_shapes=[pltpu.VMEM((ATTN_HEADS, CHUNK, BAND), F32),
                        pltpu.VMEM((KV_HALO + BLOCK, 2 * D_KV), BF16),
                        pltpu.VMEM((POOL_HALO + BLOCK, D_POOL), F32),
                        pltpu.VMEM((BLOCK, D_POOL), BF16),
                        pltpu.VMEM((MIX_RING, BLOCK, D), F32),
                        pltpu.VMEM((MIX_RING, BLOCK, D), F32),
                        pltpu.VMEM((EXPERT_RING, sub_rows * (row_tiles + 1), LANES), F32),
                        pltpu.SemaphoreType.DMA((EXPERT_RING,)),
                        pltpu.SMEM((PEER_SLOTS, 2 * BLOCK), jnp.int32),
                        pltpu.VMEM((2, PEER_SLOTS, BLOCK), jnp.int32),
                        pltpu.VMEM((ROUTE_AHEAD + 1, PEER_SLOTS, BLOCK), F32),
                        pltpu.VMEM((PEER_TOPK, BLOCK), F32),
                        pltpu.VMEM((PEER_TOPK, BLOCK), F32),
                        pltpu.SemaphoreType.DMA((1,))],
        compiler_params=pltpu.CompilerParams(dimension_semantics=("arbitrary",),
                                             vmem_limit_bytes=VMEM_LIMIT),
        name="block_pipeline",
    )(x2, q, kv, kv, p, p, jnp.asarray(_t5_bucket_table()), rel_bias.astype(F32),
      sinks.astype(F32), pool_w.astype(BF16), pool_scale.reshape(1, D_POOL),
      w_out.astype(BF16), g2.reshape(1, D), wq, sk, jnp.asarray(pos), jnp.asarray(pen),
      table, table_slabs)


def kernel(x, norm1_g, w_in, q_norm_g, k_norm_g, attn_sinks, rel_bias, pool_w, pool_scale, w_out,
           norm2_g, peer_wq, peer_subkeys, peer_u, peer_v):
    batch, seq, d_model = x.shape
    depth = norm1_g.shape[0]
    assert (batch * seq) % IN_PROJ_ROWS == 0
    x2 = x.reshape(batch * seq, d_model)
    for l in range(depth):
        q, kv, p, table = _in_proj(x2, norm1_g[l], w_in[l], q_norm_g[l], k_norm_g[l],
                                   peer_u[l], peer_v[l])
        x2 = _block_pipeline(x2, q, kv, p, rel_bias, attn_sinks[l], pool_w[l], pool_scale[l],
                             w_out[l], norm2_g[l], peer_wq[l], peer_subkeys[l], table, seq)
    return x2.reshape(batch, seq, d_model)
```
